```python
import jax, jax.numpy as jnp
from jax import lax
import numpy as np

D_MODEL = 1024
BATCH = 16
SEQ = 2048
DEPTH = 1

D_MIX = D_MODEL
ATTN_HEADS = 8
HEAD_DIM = 64
D_ATTN = ATTN_HEADS * HEAD_DIM
D_CONV = D_MIX - D_ATTN
CONV_GROUPS = 8
CONV_K = 31
DILATED_PATTERNS = ((128, 1), (512, 4), (2048, 16))
BLK = 128
D_FF = 2816
D_IN = 3 * D_ATTN + 2 * D_CONV
EPS = 1e-6

kernel_name = "hybrid_dilated_attn_conformer_conv_macaron"


def _rms(x, g):
    xf = x.astype(jnp.float32)
    y = xf * lax.rsqrt(jnp.mean(xf * xf, axis=-1, keepdims=True) + EPS)
    return (y * g.astype(jnp.float32)).astype(x.dtype)


def _layernorm(x, g, b):
    xf = x.astype(jnp.float32)
    mu = jnp.mean(xf, axis=-1, keepdims=True)
    var = jnp.mean(jnp.square(xf - mu), axis=-1, keepdims=True)
    y = (xf - mu) * lax.rsqrt(var + EPS)
    return (y * g.astype(jnp.float32) + b.astype(jnp.float32)).astype(x.dtype)


def _swiglu(x, w_gate, w_up, w_down):
    return (jax.nn.silu(x @ w_gate) * (x @ w_up)) @ w_down


def _banded_causal_attn(q, k, v, steps):
    Bp, L, H, Dh = q.shape
    N = L // BLK
    qb = q.reshape(Bp, N, BLK, H, Dh)
    kb = k.reshape(Bp, N, BLK, H, Dh)
    vb = v.reshape(Bp, N, BLK, H, Dh)

    def with_prev(a):
        prev = jnp.concatenate([jnp.zeros_like(a[:, :1]), a[:, :-1]], axis=1)
        return jnp.concatenate([prev, a], axis=2)

    kk, vv = with_prev(kb), with_prev(vb)
    s = jnp.einsum('bnqhd,bnkhd->bnhqk', qb, kk,
                   preferred_element_type=jnp.float32) * (Dh ** -0.5)
    qi = jnp.arange(BLK)[:, None]
    ci = jnp.arange(2 * BLK)[None, :]
    dist = BLK + qi - ci
    band = (dist >= 0) & (dist <= steps)
    first = (jnp.arange(N) == 0)[:, None, None] & (ci < BLK)[None]
    mask = band[None] & jnp.logical_not(first)
    s = jnp.where(mask[None, :, None], s, jnp.float32(-1e30))
    m = jnp.max(s, axis=-1, keepdims=True)
    p = jnp.exp(s - m)
    l = jnp.sum(p, axis=-1, keepdims=True)
    o = jnp.einsum('bnhqk,bnkhd->bnqhd', (p / l).astype(v.dtype), vv,
                   preferred_element_type=jnp.float32)
    lse = (m + jnp.log(l))[..., 0]
    return o.reshape(Bp, L, H, Dh), lse.transpose(0, 1, 3, 2).reshape(Bp, L, H)


def _dilated_causal_attn(q, k, v, window, dilation):
    B, S, H, Dh = q.shape
    span = dilation * BLK
    S_pad = -(-S // span) * span
    L = S_pad // dilation

    def strided(a):
        a = jnp.pad(a, ((0, 0), (0, S_pad - S), (0, 0), (0, 0)))
        return a.reshape(B, L, dilation, H, Dh).transpose(0, 2, 1, 3, 4).reshape(B * dilation, L, H, Dh)

    o, lse = _banded_causal_attn(strided(q), strided(k), strided(v), window // dilation)
    o = o.reshape(B, dilation, L, H, Dh).transpose(0, 2, 1, 3, 4).reshape(B, S_pad, H, Dh)[:, :S]
    lse = lse.reshape(B, dilation, L, H).transpose(0, 2, 1, 3).reshape(B, S_pad, H)[:, :S]
    return o, lse


def _conformer_conv(a, gate, conv_w, conv_b, ln_g, ln_b):
    glu = a * jax.nn.sigmoid(gate)
    y = lax.conv_general_dilated(
        glu, conv_w[:, None, :], window_strides=(1,), padding=[(CONV_K - 1, 0)],
        dimension_numbers=('NWC', 'WIO', 'NWC'), feature_group_count=D_CONV)
    y = y + conv_b
    return jax.nn.silu(_layernorm(y, ln_g, ln_b))


def setup_inputs(seed: int = 0) -> dict:
    key = jax.random.key(seed)
    ks = jax.random.split(key, 20)
    L = DEPTH

    def nrm(k, shape, scale):
        return jax.random.normal(k, shape, jnp.float32) * scale

    def gain(k, shape):
        return 1.0 + 0.02 * jax.random.normal(k, shape, jnp.float32)

    return {
        "x": jax.random.normal(ks[0], (BATCH, SEQ, D_MODEL), jnp.float32),
        "ffn1_norm": gain(ks[1], (L, D_MODEL)),
        "ffn1_w_gate": nrm(ks[2], (L, D_MODEL, D_FF), D_MODEL ** -0.5),
        "ffn1_w_up": nrm(ks[3], (L, D_MODEL, D_FF), D_MODEL ** -0.5),
        "ffn1_w_down": nrm(ks[4], (L, D_FF, D_MODEL), D_FF ** -0.5),
        "mix_norm": gain(ks[5], (L, D_MODEL)),
        "w_in": nrm(ks[6], (L, D_MODEL, D_IN), D_MODEL ** -0.5),
        "q_norm": gain(ks[7], (L, HEAD_DIM)),
        "k_norm": gain(ks[8], (L, HEAD_DIM)),
        "conv_w": nrm(ks[9], (L, CONV_K, D_CONV), CONV_K ** -0.5),
        "conv_b": nrm(ks[10], (L, D_CONV), 0.02),
        "conv_ln_g": gain(ks[11], (L, D_CONV)),
        "conv_ln_b": nrm(ks[12], (L, D_CONV), 0.02),
        "w_out": nrm(ks[13], (L, D_MIX, D_MODEL), D_MIX ** -0.5),
        "ffn2_norm": gain(ks[14], (L, D_MODEL)),
        "ffn2_w_gate": nrm(ks[15], (L, D_MODEL, D_FF), D_MODEL ** -0.5),
        "ffn2_w_up": nrm(ks[16], (L, D_MODEL, D_FF), D_MODEL ** -0.5),
        "ffn2_w_down": nrm(ks[17], (L, D_FF, D_MODEL), D_FF ** -0.5),
    }


def reference(x, ffn1_norm, ffn1_w_gate, ffn1_w_up, ffn1_w_down, mix_norm, w_in, q_norm, k_norm,
              conv_w, conv_b, conv_ln_g, conv_ln_b, w_out, ffn2_norm, ffn2_w_gate, ffn2_w_up,
              ffn2_w_down):
    B, S, _ = x.shape
    h = x
    for l in range(DEPTH):
        h = h + 0.5 * _swiglu(_rms(h, ffn1_norm[l]), ffn1_w_gate[l], ffn1_w_up[l], ffn1_w_down[l])

        u = _rms(h, mix_norm[l]) @ w_in[l]
        q, k, v, ca, cg = jnp.split(
            u, np.cumsum([D_ATTN, D_ATTN, D_ATTN, D_CONV]).tolist(), axis=-1)

        q = _rms(q.reshape(B, S, ATTN_HEADS, HEAD_DIM), q_norm[l])
        k = _rms(k.reshape(B, S, ATTN_HEADS, HEAD_DIM), k_norm[l])
        v = v.reshape(B, S, ATTN_HEADS, HEAD_DIM)
        outs, lses = [], []
        for window, dilation in DILATED_PATTERNS:
            o, lse = _dilated_causal_attn(q, k, v, window, dilation)
            outs.append(o)
            lses.append(lse)
        wts = jax.nn.softmax(jnp.stack(lses, axis=0), axis=0)
        attn = jnp.sum(wts[..., None] * jnp.stack(outs, axis=0), axis=0)
        attn = attn.reshape(B, S, D_ATTN).astype(h.dtype)

        conv = _conformer_conv(ca, cg, conv_w[l], conv_b[l], conv_ln_g[l], conv_ln_b[l])

        h = h + jnp.concatenate([attn, conv], axis=-1) @ w_out[l]

        h = h + 0.5 * _swiglu(_rms(h, ffn2_norm[l]), ffn2_w_gate[l], ffn2_w_up[l], ffn2_w_down[l])
    return h
```

```python
import functools

import jax
import jax.numpy as jnp
from jax import lax
from jax.experimental import pallas as pl
from jax.experimental.pallas import tpu as pltpu

F32 = jnp.float32
BF16 = jnp.bfloat16

D_MODEL = 1024
D_FF = 2816
HEADS = 8
HEAD_DIM = 64
D_ATTN = HEADS * HEAD_DIM
D_CONV = 512
D_IN = 3 * D_ATTN + 2 * D_CONV
CONV_K = 31
BLK = 128
DILATIONS = (1, 4, 16)
EPS = 1e-6
NEG = -1e30

LANES = 128
PAIR = 2 * HEAD_DIM
N_PAIRS = D_ATTN // PAIR
HALO = 32

VMEM_LIMIT_BYTES = 56 * 1024 * 1024

TM_FFN = 512
TS_CONV = 256
CONV_CHUNK = 64


def _const_spec(shape):
    return pl.BlockSpec(shape, lambda *_: (0,) * len(shape), pipeline_mode=pl.Buffered(1))


def _rms(x, g):
    ms = jnp.mean(x * x, axis=-1, keepdims=True)
    return x * lax.rsqrt(ms + EPS) * g


def _swiglu(xn, wg_ref, wu_ref, wd_ref):
    g = jnp.dot(xn, wg_ref[...], preferred_element_type=F32)
    u = jnp.dot(xn, wu_ref[...], preferred_element_type=F32)
    a = (g * jax.nn.sigmoid(g) * u).astype(BF16)
    return jnp.dot(a, wd_ref[...], preferred_element_type=F32)


def _ffn_inproj_kernel(x_ref, n1_ref, wg_ref, wu_ref, wd_ref, nm_ref, win_ref, qg_ref, kg_ref,
                       hm_ref, h1_ref, q_ref, k_ref, v_ref, glu_ref):
    x = x_ref[...]
    h1 = x + 0.5 * _swiglu(_rms(x, n1_ref[...]).astype(BF16), wg_ref, wu_ref, wd_ref)
    h1_ref[...] = h1

    u = jnp.dot(_rms(h1, nm_ref[...]).astype(BF16), win_ref[...], preferred_element_type=F32)
    q = u[:, 0:D_ATTN]
    k = u[:, D_ATTN:2 * D_ATTN]
    v_ref[...] = u[:, 2 * D_ATTN:3 * D_ATTN]
    ca = u[:, 3 * D_ATTN:3 * D_ATTN + D_CONV]
    cg = u[:, 3 * D_ATTN + D_CONV:]
    glu_ref[...] = ca * jax.nn.sigmoid(cg)

    hm = hm_ref[...]
    q_ms = jnp.dot((q * q).astype(BF16), hm, preferred_element_type=F32)
    k_ms = jnp.dot((k * k).astype(BF16), hm, preferred_element_type=F32)
    q_ref[...] = q * lax.rsqrt(q_ms + EPS) * (qg_ref[...] * (HEAD_DIM ** -0.5))
    k_ref[...] = k * lax.rsqrt(k_ms + EPS) * kg_ref[...]


def _ffn_inproj(x2, n1, wg, wu, wd, nm, win, qg, kg, hm):
    T = x2.shape[0]
    tm = TM_FFN
    row = lambda w: pl.BlockSpec((tm, w), lambda i: (i, 0))
    return pl.pallas_call(
        _ffn_inproj_kernel,
        grid=(T // tm,),
        in_specs=[row(D_MODEL), _const_spec((1, D_MODEL)),
                  _const_spec((D_MODEL, D_FF)), _const_spec((D_MODEL, D_FF)), _const_spec((D_FF, D_MODEL)),
                  _const_spec((1, D_MODEL)), _const_spec((D_MODEL, D_IN)),
                  _const_spec((1, D_ATTN)), _const_spec((1, D_ATTN)), _const_spec((D_ATTN, D_ATTN))],
        out_specs=[row(D_MODEL), row(D_ATTN), row(D_ATTN), row(D_ATTN), row(D_CONV)],
        out_shape=[jax.ShapeDtypeStruct((T, D_MODEL), F32)] + [jax.ShapeDtypeStruct((T, D_ATTN), F32)] * 4,
        compiler_params=pltpu.CompilerParams(dimension_semantics=("arbitrary",),
                                             vmem_limit_bytes=VMEM_LIMIT_BYTES),
        name="ffn_inproj",
    )(x2, n1, wg, wu, wd, nm, win, qg, kg, hm)


def _rows(ref, start, dilation):
    if dilation == 1:
        return ref[pl.ds(start, BLK), :]
    return ref[pl.ds(start, BLK, stride=dilation), :]


def _store_rows(ref, start, dilation, val):
    if dilation == 1:
        ref[pl.ds(start, BLK), :] = val
    else:
        ref[pl.ds(start, BLK, stride=dilation), :] = val


def _fill_band_mask(mask_ref):
    qi = lax.broadcasted_iota(jnp.int32, (2 * BLK, 2 * BLK), 0) & (BLK - 1)
    ci = lax.broadcasted_iota(jnp.int32, (2 * BLK, 2 * BLK), 1)
    dist = BLK + qi - ci
    mask_ref[...] = jnp.where((dist >= 0) & (dist <= BLK), 0.0, NEG)


def _attn_block(q_ref, k_ref, v_ref, o_ref, l_ref, mask_ref, start, prev_start, dilation, has_prev):
    lane = lax.broadcasted_iota(jnp.int32, (BLK, PAIR), 1)
    first = lane < HEAD_DIM

    q = _rows(q_ref, start, dilation)
    q2 = jnp.concatenate([jnp.where(first, q, 0.0), jnp.where(first, 0.0, q)], axis=0).astype(BF16)
    k_all = _rows(k_ref, start, dilation).astype(BF16)
    v_all = _rows(v_ref, start, dilation).astype(BF16)
    if has_prev:
        k_all = jnp.concatenate([_rows(k_ref, prev_start, dilation).astype(BF16), k_all], axis=0)
        v_all = jnp.concatenate([_rows(v_ref, prev_start, dilation).astype(BF16), v_all], axis=0)
        mask = mask_ref[...]
    else:
        mask = mask_ref[:, BLK:]

    s = lax.dot_general(q2, k_all, (((1,), (1,)), ((), ())), preferred_element_type=F32)
    s = s + mask
    m = jnp.max(s, axis=-1, keepdims=True)
    p = jnp.exp(s - m)
    l = jnp.sum(p, axis=-1, keepdims=True)
    pv = jnp.dot(p.astype(BF16), v_all, preferred_element_type=F32) / l
    lse = m + jnp.log(l)
    _store_rows(o_ref, start, dilation, jnp.where(first, pv[:BLK], pv[BLK:]))
    _store_rows(l_ref, start, dilation,
                jnp.where(first, jnp.broadcast_to(lse[:BLK], (BLK, PAIR)),
                          jnp.broadcast_to(lse[BLK:], (BLK, PAIR))))


def _attn_kernel(q_ref, k_ref, v_ref, out_ref, mask_ref, o1, l1, o4, l4, o16, l16):
    seq = q_ref.shape[0]
    _fill_band_mask(mask_ref)
    for dilation, o_ref, l_ref in zip(DILATIONS, (o1, o4, o16), (l1, l4, l16)):
        span = BLK * dilation
        n_blocks = seq // span
        for r in range(dilation):
            _attn_block(q_ref, k_ref, v_ref, o_ref, l_ref, mask_ref, r, 0, dilation, has_prev=False)
            if n_blocks > 1:
                def body(n, carry, r=r, dilation=dilation, span=span, o_ref=o_ref, l_ref=l_ref):
                    start = pl.multiple_of(n * span, span) + r
                    _attn_block(q_ref, k_ref, v_ref, o_ref, l_ref, mask_ref, start, start - span, dilation,
                                has_prev=True)
                    return carry
                lax.fori_loop(1, n_blocks, body, 0)

    la, lb, lc = l1[...], l4[...], l16[...]
    m = jnp.maximum(jnp.maximum(la, lb), lc)
    wa, wb, wc = jnp.exp(la - m), jnp.exp(lb - m), jnp.exp(lc - m)
    out = (wa * o1[...] + wb * o4[...] + wc * o16[...]) / (wa + wb + wc)
    out_ref[...] = out.astype(out_ref.dtype)


def _dilated_attn(q, k, v):
    B, S, _ = q.shape
    spec = pl.BlockSpec((None, S, PAIR), lambda b, h: (b, 0, h))
    return pl.pallas_call(
        _attn_kernel,
        grid=(B, N_PAIRS),
        in_specs=[spec, spec, spec],
        out_specs=spec,
        out_shape=jax.ShapeDtypeStruct((B, S, D_ATTN), BF16),
        scratch_shapes=[pltpu.VMEM((2 * BLK, 2 * BLK), F32)] + [pltpu.VMEM((S, PAIR), F32)] * 6,
        compiler_params=pltpu.CompilerParams(dimension_semantics=("arbitrary", "arbitrary"),
                                             vmem_limit_bytes=VMEM_LIMIT_BYTES),
        name="dilated_attn",
    )(q, k, v)


def _conv_kernel(cur_ref, halo_ref, w_ref, b_ref, g_ref, beta_ref, out_ref, pad_ref):
    ts = cur_ref.shape[0]
    pad_ref[0:HALO, :] = jnp.where(pl.program_id(1) > 0, halo_ref[...], 0.0)
    pad_ref[HALO:, :] = cur_ref[...]
    shift = HALO - (CONV_K - 1)
    for c in range(ts // CONV_CHUNK):
        base = c * CONV_CHUNK + shift
        acc = jnp.zeros((CONV_CHUNK, D_CONV), F32)
        for j in range(CONV_K):
            acc = acc + pad_ref[base + j:base + j + CONV_CHUNK, :] * w_ref[j:j + 1, :]
        y = acc + b_ref[...]
        mu = jnp.mean(y, axis=-1, keepdims=True)
        yc = y - mu
        var = jnp.mean(yc * yc, axis=-1, keepdims=True)
        z = yc * lax.rsqrt(var + EPS) * g_ref[...] + beta_ref[...]
        out_ref[c * CONV_CHUNK:(c + 1) * CONV_CHUNK, :] = (z * jax.nn.sigmoid(z)).astype(out_ref.dtype)


def _conv_ln(glu, w, b, g, beta):
    B, S, C = glu.shape
    ts = TS_CONV
    per = ts // HALO
    return pl.pallas_call(
        _conv_kernel,
        grid=(B, S // ts),
        in_specs=[pl.BlockSpec((None, ts, C), lambda bi, i: (bi, i, 0)),
                  pl.BlockSpec((None, HALO, C), lambda bi, i: (bi, jnp.maximum(i * per - 1, 0), 0)),
                  _const_spec((CONV_K, C)), _const_spec((1, C)), _const_spec((1, C)), _const_spec((1, C))],
        out_specs=pl.BlockSpec((None, ts, C), lambda bi, i: (bi, i, 0)),
        out_shape=jax.ShapeDtypeStruct((B, S, C), BF16),
        scratch_shapes=[pltpu.VMEM((ts + HALO, C), F32)],
        compiler_params=pltpu.CompilerParams(dimension_semantics=("arbitrary", "arbitrary"),
                                             vmem_limit_bytes=VMEM_LIMIT_BYTES),
        name="conv_ln",
    )(glu, glu, w, b, g, beta)


def _outproj_ffn_kernel(h1_ref, attn_ref, conv_ref, woa_ref, woc_ref, n2_ref, wg_ref, wu_ref, wd_ref,
                        out_ref):
    mix = (jnp.dot(attn_ref[...], woa_ref[...], preferred_element_type=F32)
           + jnp.dot(conv_ref[...], woc_ref[...], preferred_element_type=F32))
    h2 = h1_ref[...] + mix
    out_ref[...] = h2 + 0.5 * _swiglu(_rms(h2, n2_ref[...]).astype(BF16), wg_ref, wu_ref, wd_ref)


def _outproj_ffn(h1, attn, conv, woa, woc, n2, wg, wu, wd):
    T = h1.shape[0]
    tm = TM_FFN
    row = lambda w: pl.BlockSpec((tm, w), lambda i: (i, 0))
    return pl.pallas_call(
        _outproj_ffn_kernel,
        grid=(T // tm,),
        in_specs=[row(D_MODEL), row(D_ATTN), row(D_CONV),
                  _const_spec((D_ATTN, D_MODEL)), _const_spec((D_CONV, D_MODEL)), _const_spec((1, D_MODEL)),
                  _const_spec((D_MODEL, D_FF)), _const_spec((D_MODEL, D_FF)), _const_spec((D_FF, D_MODEL))],
        out_specs=row(D_MODEL),
        out_shape=jax.ShapeDtypeStruct((T, D_MODEL), F32),
        compiler_params=pltpu.CompilerParams(dimension_semantics=("arbitrary",),
                                             vmem_limit_bytes=VMEM_LIMIT_BYTES),
        name="outproj_ffn",
    )(h1, attn, conv, woa, woc, n2, wg, wu, wd)


def _layer(h, ffn1_norm, ffn1_w_gate, ffn1_w_up, ffn1_w_down, mix_norm, w_in, q_norm, k_norm, conv_w,
           conv_b, conv_ln_g, conv_ln_b, w_out, ffn2_norm, ffn2_w_gate, ffn2_w_up, ffn2_w_down):
    B, S, D = h.shape
    T = B * S
    row = lambda a: a.reshape(1, -1).astype(F32)
    head_id = jnp.arange(D_ATTN) // HEAD_DIM
    head_mean = jnp.where(head_id[:, None] == head_id[None, :], 1.0 / HEAD_DIM, 0.0).astype(BF16)

    h1, q, k, v, glu = _ffn_inproj(
        h.reshape(T, D), row(ffn1_norm), ffn1_w_gate.astype(BF16), ffn1_w_up.astype(BF16),
        ffn1_w_down.astype(BF16), row(mix_norm), w_in.astype(BF16),
        row(jnp.tile(q_norm, HEADS)), row(jnp.tile(k_norm, HEADS)), head_mean)

    attn = _dilated_attn(q.reshape(B, S, D_ATTN), k.reshape(B, S, D_ATTN), v.reshape(B, S, D_ATTN))
    conv = _conv_ln(glu.reshape(B, S, D_CONV), conv_w.astype(F32), row(conv_b), row(conv_ln_g),
                    row(conv_ln_b))

    w_out16 = w_out.astype(BF16)
    out = _outproj_ffn(h1, attn.reshape(T, D_ATTN), conv.reshape(T, D_CONV), w_out16[:D_ATTN],
                       w_out16[D_ATTN:], row(ffn2_norm), ffn2_w_gate.astype(BF16), ffn2_w_up.astype(BF16),
                       ffn2_w_down.astype(BF16))
    return out.reshape(B, S, D)


def kernel(x, ffn1_norm, ffn1_w_gate, ffn1_w_up, ffn1_w_down, mix_norm, w_in, q_norm, k_norm, conv_w, conv_b,
           conv_ln_g, conv_ln_b, w_out, ffn2_norm, ffn2_w_gate, ffn2_w_up, ffn2_w_down):
    assert x.shape[1] % (BLK * max(DILATIONS)) == 0 and x.shape[2] == D_MODEL
    h = x
    for l in range(ffn1_norm.shape[0]):
        h = _layer(h, ffn1_norm[l], ffn1_w_gate[l], ffn1_w_up[l], ffn1_w_down[l], mix_norm[l], w_in[l],
                   q_norm[l], k_norm[l], conv_w[l], conv_b[l], conv_ln_g[l], conv_ln_b[l], w_out[l],
                   ffn2_norm[l], ffn2_w_gate[l], ffn2_w_up[l], ffn2_w_down[l])
    return h
```

```python
import functools

import jax
import jax.numpy as jnp
from jax import lax
from jax.experimental import pallas as pl
from jax.experimental.pallas import tpu as pltpu

F32 = jnp.float32
BF16 = jnp.bfloat16

D_MODEL = 1024
D_FF = 2816
HEADS = 8
HEAD_DIM = 64
D_ATTN = HEADS * HEAD_DIM
D_CONV = 512
D_IN = 3 * D_ATTN + 2 * D_CONV
CONV_K = 31
BLK = 128
DILATIONS = (1, 4, 16)
EPS = 1e-6
NEG = -1e30

LANES = 128
SUBLANES = 8
PAIR = 2 * HEAD_DIM
N_PAIRS = D_ATTN // PAIR
HALO = 32

VMEM_LIMIT_BYTES = 56 * 1024 * 1024

TM_FFN = 512
TS_CONV = 256
CONV_CHUNK = 64
ATTN_GROUP = 4


def _const_spec(shape):
    return pl.BlockSpec(shape, lambda *_: (0,) * len(shape), pipeline_mode=pl.Buffered(1))


def _rms(x, g):
    ms = jnp.mean(x * x, axis=-1, keepdims=True)
    return x * lax.rsqrt(ms + EPS) * g


def _swiglu(xn, wg_ref, wu_ref, wd_ref):
    g = jnp.dot(xn, wg_ref[...], preferred_element_type=F32)
    u = jnp.dot(xn, wu_ref[...], preferred_element_type=F32)
    a = (g * jax.nn.sigmoid(g) * u).astype(BF16)
    return jnp.dot(a, wd_ref[...], preferred_element_type=F32)


def _ffn_inproj_kernel(x_ref, n1_ref, wg_ref, wu_ref, wd_ref, nm_ref, win_ref, qg_ref, kg_ref,
                       hm_ref, h1_ref, q_ref, k_ref, v_ref, glu_ref):
    x = x_ref[...]
    h1 = x + 0.5 * _swiglu(_rms(x, n1_ref[...]).astype(BF16), wg_ref, wu_ref, wd_ref)
    h1_ref[...] = h1

    u = jnp.dot(_rms(h1, nm_ref[...]).astype(BF16), win_ref[...], preferred_element_type=F32)
    q = u[:, 0:D_ATTN]
    k = u[:, D_ATTN:2 * D_ATTN]
    v_ref[...] = u[:, 2 * D_ATTN:3 * D_ATTN]
    ca = u[:, 3 * D_ATTN:3 * D_ATTN + D_CONV]
    cg = u[:, 3 * D_ATTN + D_CONV:]
    glu_ref[...] = ca * jax.nn.sigmoid(cg)

    hm = hm_ref[...]
    q_ms = jnp.dot((q * q).astype(BF16), hm, preferred_element_type=F32)
    k_ms = jnp.dot((k * k).astype(BF16), hm, preferred_element_type=F32)
    q_ref[...] = q * lax.rsqrt(q_ms + EPS) * (qg_ref[...] * (HEAD_DIM ** -0.5))
    k_ref[...] = k * lax.rsqrt(k_ms + EPS) * kg_ref[...]


def _ffn_inproj(x2, n1, wg, wu, wd, nm, win, qg, kg, hm):
    T = x2.shape[0]
    tm = TM_FFN
    row = lambda w: pl.BlockSpec((tm, w), lambda i: (i, 0))
    return pl.pallas_call(
        _ffn_inproj_kernel,
        grid=(T // tm,),
        in_specs=[row(D_MODEL), _const_spec((1, D_MODEL)),
                  _const_spec((D_MODEL, D_FF)), _const_spec((D_MODEL, D_FF)), _const_spec((D_FF, D_MODEL)),
                  _const_spec((1, D_MODEL)), _const_spec((D_MODEL, D_IN)),
                  _const_spec((1, D_ATTN)), _const_spec((1, D_ATTN)), _const_spec((D_ATTN, D_ATTN))],
        out_specs=[row(D_MODEL), row(D_ATTN), row(D_ATTN), row(D_ATTN), row(D_CONV)],
        out_shape=[jax.ShapeDtypeStruct((T, D_MODEL), F32)] + [jax.ShapeDtypeStruct((T, D_ATTN), F32)] * 4,
        compiler_params=pltpu.CompilerParams(dimension_semantics=("arbitrary",),
                                             vmem_limit_bytes=VMEM_LIMIT_BYTES),
        name="ffn_inproj",
    )(x2, n1, wg, wu, wd, nm, win, qg, kg, hm)


def _rows(ref, start, dilation):
    if dilation == 1:
        return ref[pl.ds(start, BLK), :]
    return ref[pl.ds(start, BLK, stride=dilation), :]


def _store_rows(ref, start, dilation, val):
    if dilation == 1:
        ref[pl.ds(start, BLK), :] = val
    else:
        ref[pl.ds(start, BLK, stride=dilation), :] = val


def _fill_band_mask(mask_ref):
    qi = lax.broadcasted_iota(jnp.int32, (2 * BLK, 2 * BLK), 0) & (BLK - 1)
    ci = lax.broadcasted_iota(jnp.int32, (2 * BLK, 2 * BLK), 1)
    dist = BLK + qi - ci
    mask_ref[...] = jnp.where((dist >= 0) & (dist <= BLK), 0.0, NEG)


def _attn_chain(refs, start0, dilation, chain, has_prev):
    q_ref, k_ref, v_ref, o_ref, l_ref, mask_ref = refs
    span = BLK * dilation
    lane = lax.broadcasted_iota(jnp.int32, (BLK, PAIR), 1)
    first = lane < HEAD_DIM
    lo = -1 if has_prev else 0
    k_t = [_rows(k_ref, start0 + j * span, dilation).astype(BF16) for j in range(lo, chain)]
    v_t = [_rows(v_ref, start0 + j * span, dilation).astype(BF16) for j in range(lo, chain)]
    for j in range(chain):
        start = start0 + j * span
        q = _rows(q_ref, start, dilation)
        q2 = jnp.concatenate([jnp.where(first, q, 0.0), jnp.where(first, 0.0, q)], axis=0).astype(BF16)
        t = j - lo
        if t > 0:
            k_all = jnp.concatenate([k_t[t - 1], k_t[t]], axis=0)
            v_all = jnp.concatenate([v_t[t - 1], v_t[t]], axis=0)
            mask = mask_ref[...]
        else:
            k_all, v_all, mask = k_t[0], v_t[0], mask_ref[:, BLK:]
        v_aug = jnp.concatenate([v_all, jnp.ones(v_all.shape, BF16)], axis=1)

        s = lax.dot_general(q2, k_all, (((1,), (1,)), ((), ())), preferred_element_type=F32)
        s = s + mask
        m = jnp.max(s, axis=-1, keepdims=True)
        p = jnp.exp(s - m).astype(BF16)
        pv = jnp.dot(p, v_aug, preferred_element_type=F32)
        l = pv[:, PAIR:]
        o = pv[:, :PAIR] / l
        lse = m + jnp.log(l)
        _store_rows(o_ref, start, dilation, jnp.where(first, o[:BLK], o[BLK:]))
        _store_rows(l_ref, start, dilation, jnp.where(first, lse[:BLK], lse[BLK:]))


def _attn_pattern(refs, seq, dilation):
    span = BLK * dilation
    n_blocks = seq // span
    chain = min(n_blocks, ATTN_GROUP)
    for r in range(dilation):
        for c in range(n_blocks // chain):
            _attn_chain(refs, c * chain * span + r, dilation, chain, has_prev=c > 0)


def _attn_kernel(q_ref, k_ref, v_ref, out_ref, mask_ref, o1, l1, o4, l4, o16, l16):
    seq = q_ref.shape[0]
    _fill_band_mask(mask_ref)
    for dilation, o_ref, l_ref in zip(DILATIONS, (o1, o4, o16), (l1, l4, l16)):
        _attn_pattern((q_ref, k_ref, v_ref, o_ref, l_ref, mask_ref), seq, dilation)

    la, lb, lc = l1[...], l4[...], l16[...]
    m = jnp.maximum(jnp.maximum(la, lb), lc)
    wa, wb, wc = jnp.exp(la - m), jnp.exp(lb - m), jnp.exp(lc - m)
    out = (wa * o1[...] + wb * o4[...] + wc * o16[...]) / (wa + wb + wc)
    out_ref[...] = out.astype(out_ref.dtype)


def _dilated_attn(q, k, v):
    B, S, _ = q.shape
    spec = pl.BlockSpec((None, S, PAIR), lambda b, h: (b, 0, h))
    return pl.pallas_call(
        _attn_kernel,
        grid=(B, N_PAIRS),
        in_specs=[spec, spec, spec],
        out_specs=spec,
        out_shape=jax.ShapeDtypeStruct((B, S, D_ATTN), BF16),
        scratch_shapes=[pltpu.VMEM((2 * BLK, 2 * BLK), F32)] + [pltpu.VMEM((S, PAIR), F32)] * 6,
        compiler_params=pltpu.CompilerParams(dimension_semantics=("arbitrary", "arbitrary"),
                                             vmem_limit_bytes=VMEM_LIMIT_BYTES),
        name="dilated_attn",
    )(q, k, v)


def _conv_taps(pad_ref, w_ref, row0, lanes):
    first = HALO - (CONV_K - 1)
    y = None
    for b in range(SUBLANES):
        rows = CONV_CHUNK + (SUBLANES if b else 0)
        u = None
        for a in range(HALO // SUBLANES + 1):
            j = SUBLANES * a + b - first
            if 0 <= j < CONV_K:
                lo = row0 + SUBLANES * a
                term = pad_ref[lo:lo + rows, lanes] * w_ref[j:j + 1, lanes]
                u = term if u is None else u + term
        u = u[b:b + CONV_CHUNK, :] if b else u
        y = u if y is None else y + u
    return y


def _conv_kernel(cur_ref, halo_ref, w_ref, b_ref, g_ref, beta_ref, out_ref, pad_ref, y_ref):
    ts = cur_ref.shape[0]
    pad_ref[0:HALO, :] = jnp.where(pl.program_id(1) > 0, halo_ref[...], 0.0)
    pad_ref[HALO:, :] = cur_ref[...]
    for c in range(ts // CONV_CHUNK):
        rows = slice(c * CONV_CHUNK, (c + 1) * CONV_CHUNK)
        for s in range(D_CONV // LANES):
            lanes = slice(s * LANES, (s + 1) * LANES)
            y_ref[rows, lanes] = _conv_taps(pad_ref, w_ref, c * CONV_CHUNK, lanes) + b_ref[:, lanes]
        y = y_ref[rows, :]
        mu = jnp.mean(y, axis=-1, keepdims=True)
        yc = y - mu
        var = jnp.mean(yc * yc, axis=-1, keepdims=True)
        z = yc * lax.rsqrt(var + EPS) * g_ref[...] + beta_ref[...]
        out_ref[rows, :] = (z * jax.nn.sigmoid(z)).astype(out_ref.dtype)


def _conv_ln(glu, w, b, g, beta):
    B, S, C = glu.shape
    ts = TS_CONV
    per = ts // HALO
    return pl.pallas_call(
        _conv_kernel,
        grid=(B, S // ts),
        in_specs=[pl.BlockSpec((None, ts, C), lambda bi, i: (bi, i, 0)),
                  pl.BlockSpec((None, HALO, C), lambda bi, i: (bi, jnp.maximum(i * per - 1, 0), 0)),
                  _const_spec((CONV_K, C)), _const_spec((1, C)), _const_spec((1, C)), _const_spec((1, C))],
        out_specs=pl.BlockSpec((None, ts, C), lambda bi, i: (bi, i, 0)),
        out_shape=jax.ShapeDtypeStruct((B, S, C), BF16),
        scratch_shapes=[pltpu.VMEM((ts + HALO, C), F32), pltpu.VMEM((ts, C), F32)],
        compiler_params=pltpu.CompilerParams(dimension_semantics=("arbitrary", "arbitrary"),
                                             vmem_limit_bytes=VMEM_LIMIT_BYTES),
        name="conv_ln",
    )(glu, glu, w, b, g, beta)


def _outproj_ffn_kernel(h1_ref, attn_ref, conv_ref, woa_ref, woc_ref, n2_ref, wg_ref, wu_ref, wd_ref,
                        out_ref):
    mix = (jnp.dot(attn_ref[...], woa_ref[...], preferred_element_type=F32)
           + jnp.dot(conv_ref[...], woc_ref[...], preferred_element_type=F32))
    h2 = h1_ref[...] + mix
    out_ref[...] = h2 + 0.5 * _swiglu(_rms(h2, n2_ref[...]).astype(BF16), wg_ref, wu_ref, wd_ref)


def _outproj_ffn(h1, attn, conv, woa, woc, n2, wg, wu, wd):
    T = h1.shape[0]
    tm = TM_FFN
    row = lambda w: pl.BlockSpec((tm, w), lambda i: (i, 0))
    return pl.pallas_call(
        _outproj_ffn_kernel,
        grid=(T // tm,),
        in_specs=[row(D_MODEL), row(D_ATTN), row(D_CONV),
                  _const_spec((D_ATTN, D_MODEL)), _const_spec((D_CONV, D_MODEL)), _const_spec((1, D_MODEL)),
                  _const_spec((D_MODEL, D_FF)), _const_spec((D_MODEL, D_FF)), _const_spec((D_FF, D_MODEL))],
        out_specs=row(D_MODEL),
        out_shape=jax.ShapeDtypeStruct((T, D_MODEL), F32),
        compiler_params=pltpu.CompilerParams(dimension_semantics=("arbitrary",),
                                             vmem_limit_bytes=VMEM_LIMIT_BYTES),
        name="outproj_ffn",
    )(h1, attn, conv, woa, woc, n2, wg, wu, wd)


def _layer(h, ffn1_norm, ffn1_w_gate, ffn1_w_up, ffn1_w_down, mix_norm, w_in, q_norm, k_norm, conv_w,
           conv_b, conv_ln_g, conv_ln_b, w_out, ffn2_norm, ffn2_w_gate, ffn2_w_up, ffn2_w_down):
    B, S, D = h.shape
    T = B * S
    row = lambda a: a.reshape(1, -1).astype(F32)
    head_id = jnp.arange(D_ATTN) // HEAD_DIM
    head_mean = jnp.where(head_id[:, None] == head_id[None, :], 1.0 / HEAD_DIM, 0.0).astype(BF16)

    h1, q, k, v, glu = _ffn_inproj(
        h.reshape(T, D), row(ffn1_norm), ffn1_w_gate.astype(BF16), ffn1_w_up.astype(BF16),
        ffn1_w_down.astype(BF16), row(mix_norm), w_in.astype(BF16),
        row(jnp.tile(q_norm, HEADS)), row(jnp.tile(k_norm, HEADS)), head_mean)

    attn = _dilated_attn(q.reshape(B, S, D_ATTN), k.reshape(B, S, D_ATTN), v.reshape(B, S, D_ATTN))
    conv = _conv_ln(glu.reshape(B, S, D_CONV), conv_w.astype(F32), row(conv_b), row(conv_ln_g),
                    row(conv_ln_b))

    w_out16 = w_out.astype(BF16)
    out = _outproj_ffn(h1, attn.reshape(T, D_ATTN), conv.reshape(T, D_CONV), w_out16[:D_ATTN],
                       w_out16[D_ATTN:], row(ffn2_norm), ffn2_w_gate.astype(BF16), ffn2_w_up.astype(BF16),
                       ffn2_w_down.astype(BF16))
    return out.reshape(B, S, D)


def kernel(x, ffn1_norm, ffn1_w_gate, ffn1_w_up, ffn1_w_down, mix_norm, w_in, q_norm, k_norm, conv_w, conv_b,
           conv_ln_g, conv_ln_b, w_out, ffn2_norm, ffn2_w_gate, ffn2_w_up, ffn2_w_down):
    assert x.shape[1] % (BLK * max(DILATIONS)) == 0 and x.shape[2] == D_MODEL
    h = x
    for l in range(ffn1_norm.shape[0]):
        h = _layer(h, ffn1_norm[l], ffn1_w_gate[l], ffn1_w_up[l], ffn1_w_down[l], mix_norm[l], w_in[l],
                   q_norm[l], k_norm[l], conv_w[l], conv_b[l], conv_ln_g[l], conv_ln_b[l], w_out[l],
                   ffn2_norm[l], ffn2_w_gate[l], ffn2_w_up[l], ffn2_w_down[l])
    return h
```

```python
import functools

import jax
import jax.numpy as jnp
from jax import lax
from jax.experimental import pallas as pl
from jax.experimental.pallas import tpu as pltpu

F32 = jnp.float32
BF16 = jnp.bfloat16

D_MODEL = 1024
D_FF = 2816
HEADS = 8
HEAD_DIM = 64
D_ATTN = HEADS * HEAD_DIM
D_CONV = 512
D_IN = 3 * D_ATTN + 2 * D_CONV
CONV_K = 31
BLK = 128
DILATIONS = (1, 4, 16)
EPS = 1e-6
NEG = -1e30

LANES = 128
SUBLANES = 8
PAIR = 2 * HEAD_DIM
N_PAIRS = D_ATTN // PAIR
HALO = 32

VMEM_LIMIT_BYTES = 56 * 1024 * 1024

TM_FFN = 512
FF_CHUNK = 256
CONV_CHUNK = 64
ATTN_GROUP = 4


def _const_spec(shape):
    return pl.BlockSpec(shape, lambda *_: (0,) * len(shape), pipeline_mode=pl.Buffered(1))


def _rms(x, g):
    ms = jnp.mean(x * x, axis=-1, keepdims=True)
    return x * lax.rsqrt(ms + EPS) * g


def _swiglu(xn, wg_ref, wu_ref, wd_ref, side_work=None):
    acc = None
    for c in range(D_FF // FF_CHUNK):
        cols = slice(c * FF_CHUNK, (c + 1) * FF_CHUNK)
        g = jnp.dot(xn, wg_ref[:, cols], preferred_element_type=F32)
        u = jnp.dot(xn, wu_ref[:, cols], preferred_element_type=F32)
        a = g * jax.nn.sigmoid(g) * u
        order = side_work(c) if side_work is not None else None
        if order is not None:
            a = a + jnp.concatenate([order] * (FF_CHUNK // LANES), axis=1)
        part = jnp.dot(a.astype(BF16), wd_ref[cols, :], preferred_element_type=F32)
        acc = part if acc is None else acc + part
    return acc


def _zero_after(x):
    bits = pltpu.bitcast(x, jnp.uint32)
    fold = bits[0:SUBLANES, :]
    for r in range(SUBLANES, x.shape[0], SUBLANES):
        fold = fold | bits[r:r + SUBLANES, :]
    zero = lax.shift_right_logical(lax.shift_right_logical(fold, jnp.uint32(16)), jnp.uint32(16))
    return pltpu.bitcast(zero, F32)[0:1, :]


def _conv_taps(pad_ref, w_ref, row0, lanes, order):
    first = HALO - (CONV_K - 1)
    y = None
    for b in range(SUBLANES):
        rows = CONV_CHUNK + (SUBLANES if b else 0)
        u = None
        for a in range(HALO // SUBLANES + 1):
            j = SUBLANES * a + b - first
            if 0 <= j < CONV_K:
                lo = row0 + SUBLANES * a
                term = pad_ref[lo:lo + rows, lanes] * (w_ref[j:j + 1, lanes] + order)
                u = term if u is None else u + term
        u = u[b:b + CONV_CHUNK, :] if b else u
        y = u if y is None else y + u
    return y


def _conv_rows(c, order, pad_ref, y_ref, w_ref, b_ref, g_ref, beta_ref, out_ref):
    rows = slice(c * CONV_CHUNK, (c + 1) * CONV_CHUNK)
    for s in range(D_CONV // LANES):
        lanes = slice(s * LANES, (s + 1) * LANES)
        y = _conv_taps(pad_ref, w_ref, c * CONV_CHUNK, lanes, order)
        order = _zero_after(y)
        y_ref[rows, lanes] = y + b_ref[:, lanes]
    y = y_ref[rows, :]
    mu = jnp.mean(y, axis=-1, keepdims=True)
    yc = y - mu
    var = jnp.mean(yc * yc, axis=-1, keepdims=True)
    z = yc * lax.rsqrt(var + EPS) * g_ref[...] + beta_ref[...]
    out_ref[rows, :] = (z * jax.nn.sigmoid(z)).astype(out_ref.dtype)
    return order


def _ffn_inproj_kernel(tiles_per_seq, x_ref, n1_ref, wg_ref, wu_ref, wd_ref, nm_ref, win_ref, qg_ref,
                       kg_ref, hm_ref, cw_ref, cb_ref, cg_ref, cbeta_ref,
                       h1_ref, q_ref, k_ref, v_ref, conv_ref, pad_ref, y_ref):
    i = pl.program_id(0)
    tm = x_ref.shape[0]

    @pl.when(i == 0)
    def _():
        pad_ref[...] = jnp.zeros(pad_ref.shape, F32)

    n_conv = tm // CONV_CHUNK
    assert n_conv <= D_FF // FF_CHUNK
    order = [jnp.zeros((1, LANES), F32)]

    def conv_side_work(c):
        if c >= n_conv:
            return None
        order[0] = _conv_rows(c, order[0], pad_ref, y_ref, cw_ref, cb_ref, cg_ref, cbeta_ref, conv_ref)
        return order[0]

    x = x_ref[...]
    h1 = x + 0.5 * _swiglu(_rms(x, n1_ref[...]).astype(BF16), wg_ref, wu_ref, wd_ref, conv_side_work)
    h1_ref[...] = h1

    u = jnp.dot(_rms(h1, nm_ref[...]).astype(BF16), win_ref[...], preferred_element_type=F32)
    q = u[:, 0:D_ATTN]
    k = u[:, D_ATTN:2 * D_ATTN]
    v_ref[...] = u[:, 2 * D_ATTN:3 * D_ATTN]
    ca = u[:, 3 * D_ATTN:3 * D_ATTN + D_CONV]
    cg = u[:, 3 * D_ATTN + D_CONV:]

    hm = hm_ref[...]
    q_ms = jnp.dot((q * q).astype(BF16), hm, preferred_element_type=F32)
    k_ms = jnp.dot((k * k).astype(BF16), hm, preferred_element_type=F32)
    q_ref[...] = q * lax.rsqrt(q_ms + EPS) * (qg_ref[...] * (HEAD_DIM ** -0.5))
    k_ref[...] = k * lax.rsqrt(k_ms + EPS) * kg_ref[...]

    same_seq = (i % tiles_per_seq) != 0
    pad_ref[0:HALO, :] = jnp.where(same_seq, pad_ref[tm:tm + HALO, :], 0.0)
    pad_ref[HALO:, :] = ca * jax.nn.sigmoid(cg)


def _ffn_inproj(x2, seq, n1, wg, wu, wd, nm, win, qg, kg, hm, cw, cb, cg, cbeta):
    T = x2.shape[0]
    tm = TM_FFN
    n_tiles = T // tm
    cur = lambda w: pl.BlockSpec((tm, w), lambda i: (jnp.minimum(i, n_tiles - 1), 0))
    lag = lambda w: pl.BlockSpec((tm, w), lambda i: (jnp.maximum(i - 1, 0), 0))
    return pl.pallas_call(
        functools.partial(_ffn_inproj_kernel, seq // tm),
        grid=(n_tiles + 1,),
        in_specs=[cur(D_MODEL), _const_spec((1, D_MODEL)),
                  _const_spec((D_MODEL, D_FF)), _const_spec((D_MODEL, D_FF)), _const_spec((D_FF, D_MODEL)),
                  _const_spec((1, D_MODEL)), _const_spec((D_MODEL, D_IN)),
                  _const_spec((1, D_ATTN)), _const_spec((1, D_ATTN)), _const_spec((D_ATTN, D_ATTN)),
                  _const_spec((CONV_K, D_CONV)), _const_spec((1, D_CONV)), _const_spec((1, D_CONV)),
                  _const_spec((1, D_CONV))],
        out_specs=[cur(D_MODEL), cur(D_ATTN), cur(D_ATTN), cur(D_ATTN), lag(D_CONV)],
        out_shape=[jax.ShapeDtypeStruct((T, D_MODEL), F32)] + [jax.ShapeDtypeStruct((T, D_ATTN), F32)] * 3
                  + [jax.ShapeDtypeStruct((T, D_CONV), BF16)],
        scratch_shapes=[pltpu.VMEM((tm + HALO, D_CONV), F32), pltpu.VMEM((tm, D_CONV), F32)],
        compiler_params=pltpu.CompilerParams(dimension_semantics=("arbitrary",),
                                             vmem_limit_bytes=VMEM_LIMIT_BYTES),
        name="ffn_inproj_conv",
    )(x2, n1, wg, wu, wd, nm, win, qg, kg, hm, cw, cb, cg, cbeta)


def _rows(ref, start, dilation):
    if dilation == 1:
        return ref[pl.ds(start, BLK), :]
    return ref[pl.ds(start, BLK, stride=dilation), :]


def _store_rows(ref, start, dilation, val):
    if dilation == 1:
        ref[pl.ds(start, BLK), :] = val
    else:
        ref[pl.ds(start, BLK, stride=dilation), :] = val


def _fill_band_mask(mask_ref):
    qi = lax.broadcasted_iota(jnp.int32, (2 * BLK, 2 * BLK), 0) & (BLK - 1)
    ci = lax.broadcasted_iota(jnp.int32, (2 * BLK, 2 * BLK), 1)
    dist = BLK + qi - ci
    mask_ref[...] = jnp.where((dist >= 0) & (dist <= BLK), 0.0, NEG)


def _attn_chain(refs, start0, dilation, chain, has_prev):
    q_ref, k_ref, v_ref, o_ref, l_ref, mask_ref = refs
    span = BLK * dilation
    lane = lax.broadcasted_iota(jnp.int32, (BLK, PAIR), 1)
    first = lane < HEAD_DIM
    lo = -1 if has_prev else 0
    k_t = [_rows(k_ref, start0 + j * span, dilation).astype(BF16) for j in range(lo, chain)]
    v_t = [_rows(v_ref, start0 + j * span, dilation).astype(BF16) for j in range(lo, chain)]
    for j in range(chain):
        start = start0 + j * span
        q = _rows(q_ref, start, dilation)
        q2 = jnp.concatenate([jnp.where(first, q, 0.0), jnp.where(first, 0.0, q)], axis=0).astype(BF16)
        t = j - lo
        if t > 0:
            k_all = jnp.concatenate([k_t[t - 1], k_t[t]], axis=0)
            v_all = jnp.concatenate([v_t[t - 1], v_t[t]], axis=0)
            mask = mask_ref[...]
        else:
            k_all, v_all, mask = k_t[0], v_t[0], mask_ref[:, BLK:]
        v_aug = jnp.concatenate([v_all, jnp.ones(v_all.shape, BF16)], axis=1)

        s = lax.dot_general(q2, k_all, (((1,), (1,)), ((), ())), preferred_element_type=F32)
        s = s + mask
        m = jnp.max(s, axis=-1, keepdims=True)
        p = jnp.exp(s - m).astype(BF16)
        pv = jnp.dot(p, v_aug, preferred_element_type=F32)
        l = pv[:, PAIR:]
        o = pv[:, :PAIR] / l
        lse = m + jnp.log(l)
        _store_rows(o_ref, start, dilation, jnp.where(first, o[:BLK], o[BLK:]))
        _store_rows(l_ref, start, dilation, jnp.where(first, lse[:BLK], lse[BLK:]))


def _attn_pattern(refs, seq, dilation):
    span = BLK * dilation
    n_blocks = seq // span
    chain = min(n_blocks, ATTN_GROUP)
    for r in range(dilation):
        for c in range(n_blocks // chain):
            _attn_chain(refs, c * chain * span + r, dilation, chain, has_prev=c > 0)


def _attn_kernel(q_ref, k_ref, v_ref, out_ref, mask_ref, o1, l1, o4, l4, o16, l16):
    seq = q_ref.shape[0]
    _fill_band_mask(mask_ref)
    for dilation, o_ref, l_ref in zip(DILATIONS, (o1, o4, o16), (l1, l4, l16)):
        _attn_pattern((q_ref, k_ref, v_ref, o_ref, l_ref, mask_ref), seq, dilation)

    la, lb, lc = l1[...], l4[...], l16[...]
    m = jnp.maximum(jnp.maximum(la, lb), lc)
    wa, wb, wc = jnp.exp(la - m), jnp.exp(lb - m), jnp.exp(lc - m)
    out = (wa * o1[...] + wb * o4[...] + wc * o16[...]) / (wa + wb + wc)
    out_ref[...] = out.astype(out_ref.dtype)


def _dilated_attn(q, k, v):
    B, S, _ = q.shape
    spec = pl.BlockSpec((None, S, PAIR), lambda b, h: (b, 0, h))
    return pl.pallas_call(
        _attn_kernel,
        grid=(B, N_PAIRS),
        in_specs=[spec, spec, spec],
        out_specs=spec,
        out_shape=jax.ShapeDtypeStruct((B, S, D_ATTN), BF16),
        scratch_shapes=[pltpu.VMEM((2 * BLK, 2 * BLK), F32)] + [pltpu.VMEM((S, PAIR), F32)] * 6,
        compiler_params=pltpu.CompilerParams(dimension_semantics=("arbitrary", "arbitrary"),
                                             vmem_limit_bytes=VMEM_LIMIT_BYTES),
        name="dilated_attn",
    )(q, k, v)


def _outproj_ffn_kernel(h1_ref, attn_ref, conv_ref, woa_ref, woc_ref, n2_ref, wg_ref, wu_ref, wd_ref,
                        out_ref):
    mix = (jnp.dot(attn_ref[...], woa_ref[...], preferred_element_type=F32)
           + jnp.dot(conv_ref[...], woc_ref[...], preferred_element_type=F32))
    h2 = h1_ref[...] + mix
    out_ref[...] = h2 + 0.5 * _swiglu(_rms(h2, n2_ref[...]).astype(BF16), wg_ref, wu_ref, wd_ref)


def _outproj_ffn(h1, attn, conv, woa, woc, n2, wg, wu, wd):
    T = h1.shape[0]
    tm = TM_FFN
    row = lambda w: pl.BlockSpec((tm, w), lambda i: (i, 0))
    return pl.pallas_call(
        _outproj_ffn_kernel,
        grid=(T // tm,),
        in_specs=[row(D_MODEL), row(D_ATTN), row(D_CONV),
                  _const_spec((D_ATTN, D_MODEL)), _const_spec((D_CONV, D_MODEL)), _const_spec((1, D_MODEL)),
                  _const_spec((D_MODEL, D_FF)), _const_spec((D_MODEL, D_FF)), _const_spec((D_FF, D_MODEL))],
        out_specs=row(D_MODEL),
        out_shape=jax.ShapeDtypeStruct((T, D_MODEL), F32),
        compiler_params=pltpu.CompilerParams(dimension_semantics=("arbitrary",),
                                             vmem_limit_bytes=VMEM_LIMIT_BYTES),
        name="outproj_ffn",
    )(h1, attn, conv, woa, woc, n2, wg, wu, wd)


def _layer(h, ffn1_norm, ffn1_w_gate, ffn1_w_up, ffn1_w_down, mix_norm, w_in, q_norm, k_norm, conv_w,
           conv_b, conv_ln_g, conv_ln_b, w_out, ffn2_norm, ffn2_w_gate, ffn2_w_up, ffn2_w_down):
    B, S, D = h.shape
    T = B * S
    row = lambda a: a.reshape(1, -1).astype(F32)
    head_id = jnp.arange(D_ATTN) // HEAD_DIM
    head_mean = jnp.where(head_id[:, None] == head_id[None, :], 1.0 / HEAD_DIM, 0.0).astype(BF16)

    h1, q, k, v, conv = _ffn_inproj(
        h.reshape(T, D), S, row(ffn1_norm), ffn1_w_gate.astype(BF16), ffn1_w_up.astype(BF16),
        ffn1_w_down.astype(BF16), row(mix_norm), w_in.astype(BF16),
        row(jnp.tile(q_norm, HEADS)), row(jnp.tile(k_norm, HEADS)), head_mean,
        conv_w.astype(F32), row(conv_b), row(conv_ln_g), row(conv_ln_b))

    attn = _dilated_attn(q.reshape(B, S, D_ATTN), k.reshape(B, S, D_ATTN), v.reshape(B, S, D_ATTN))

    w_out16 = w_out.astype(BF16)
    out = _outproj_ffn(h1, attn.reshape(T, D_ATTN), conv, w_out16[:D_ATTN], w_out16[D_ATTN:],
                       row(ffn2_norm), ffn2_w_gate.astype(BF16), ffn2_w_up.astype(BF16),
                       ffn2_w_down.astype(BF16))
    return out.reshape(B, S, D)


def kernel(x, ffn1_norm, ffn1_w_gate, ffn1_w_up, ffn1_w_down, mix_norm, w_in, q_norm, k_norm, conv_w, conv_b,
           conv_ln_g, conv_ln_b, w_out, ffn2_norm, ffn2_w_gate, ffn2_w_up, ffn2_w_down):
    assert x.shape[1] % (BLK * max(DILATIONS)) == 0 and x.shape[1] % TM_FFN == 0 and x.shape[2] == D_MODEL
    h = x
    for l in range(ffn1_norm.shape[0]):
        h = _layer(h, ffn1_norm[l], ffn1_w_gate[l], ffn1_w_up[l], ffn1_w_down[l], mix_norm[l], w_in[l],
                   q_norm[l], k_norm[l], conv_w[l], conv_b[l], conv_ln_g[l], conv_ln_b[l], w_out[l],
                   ffn2_norm[l], ffn2_w_gate[l], ffn2_w_up[l], ffn2_w_down[l])
    return h
```

```python
import functools

import jax
import jax.numpy as jnp
from jax import lax
from jax.experimental import pallas as pl
from jax.experimental.pallas import tpu as pltpu

F32 = jnp.float32
BF16 = jnp.bfloat16

D_MODEL = 1024
D_FF = 2816
HEADS = 8
HEAD_DIM = 64
D_ATTN = HEADS * HEAD_DIM
D_CONV = 512
D_IN = 3 * D_ATTN + 2 * D_CONV
CONV_K = 31
BLK = 128
DILATIONS = (1, 4, 16)
EPS = 1e-6
NEG = -1e30

LANES = 128
SUBLANES = 8
PAIR = 2 * HEAD_DIM
N_PAIRS = D_ATTN // PAIR
HALO = 32

VMEM_LIMIT_BYTES = 56 * 1024 * 1024

TM_FFN = 512
FF_CHUNK = 256
CONV_CHUNK = 64
ATTN_GROUP = 4


def _const_spec(shape):
    return pl.BlockSpec(shape, lambda *_: (0,) * len(shape), pipeline_mode=pl.Buffered(1))


def _rms(x, g):
    ms = jnp.mean(x * x, axis=-1, keepdims=True)
    return x * lax.rsqrt(ms + EPS) * g


def _swiglu(xn, wg_ref, wu_ref, wd_ref, side_work=None):
    def after(piece, operand):
        order = side_work(piece) if side_work is not None else None
        if order is None:
            return operand
        order = jnp.concatenate([order] * (operand.shape[1] // LANES), axis=1)
        return operand + order.astype(operand.dtype)

    acc = None
    for c in range(D_FF // FF_CHUNK):
        cols = slice(c * FF_CHUNK, (c + 1) * FF_CHUNK)
        g = jnp.dot(after(3 * c - 1, xn), wg_ref[:, cols], preferred_element_type=F32)
        u = jnp.dot(after(3 * c, xn), wu_ref[:, cols], preferred_element_type=F32)
        a = after(3 * c + 1, g * jax.nn.sigmoid(g) * u)
        part = jnp.dot(a.astype(BF16), wd_ref[cols, :], preferred_element_type=F32)
        acc = part if acc is None else acc + part
    return acc


def _zero_after(x):
    bits = pltpu.bitcast(x, jnp.uint32)
    fold = bits[0:SUBLANES, :]
    for r in range(SUBLANES, x.shape[0], SUBLANES):
        fold = fold | bits[r:r + SUBLANES, :]
    zero = lax.shift_right_logical(lax.shift_right_logical(fold, jnp.uint32(16)), jnp.uint32(16))
    return pltpu.bitcast(zero, F32)[0:1, :]


def _conv_taps(pad_ref, w_ref, row0, lanes, order):
    first = HALO - (CONV_K - 1)
    y = None
    for b in range(SUBLANES):
        rows = CONV_CHUNK + (SUBLANES if b else 0)
        u = None
        for a in range(HALO // SUBLANES + 1):
            j = SUBLANES * a + b - first
            if 0 <= j < CONV_K:
                lo = row0 + SUBLANES * a
                term = pad_ref[lo:lo + rows, lanes] * (w_ref[j:j + 1, lanes] + order)
                u = term if u is None else u + term
        u = u[b:b + CONV_CHUNK, :] if b else u
        y = u if y is None else y + u
    return y


CONV_SLABS = D_CONV // LANES


def _conv_piece(piece, order, pad_ref, y_ref, w_ref, b_ref, g_ref, beta_ref, out_ref):
    c, s = divmod(piece, CONV_SLABS)
    rows = slice(c * CONV_CHUNK, (c + 1) * CONV_CHUNK)
    lanes = slice(s * LANES, (s + 1) * LANES)
    y = _conv_taps(pad_ref, w_ref, c * CONV_CHUNK, lanes, order)
    y_ref[rows, lanes] = y + b_ref[:, lanes]
    if s == CONV_SLABS - 1:
        yr = y_ref[rows, :]
        mu = jnp.mean(yr, axis=-1, keepdims=True)
        yc = yr - mu
        var = jnp.mean(yc * yc, axis=-1, keepdims=True)
        z = yc * lax.rsqrt(var + EPS) * g_ref[...] + beta_ref[...]
        out_ref[rows, :] = (z * jax.nn.sigmoid(z)).astype(out_ref.dtype)
    return _zero_after(y)


def _ffn_inproj_kernel(tiles_per_seq, x_ref, n1_ref, wg_ref, wu_ref, wd_ref, nm_ref, win_ref, qg_ref,
                       kg_ref, hm_ref, cw_ref, cb_ref, cg_ref, cbeta_ref,
                       h1_ref, q_ref, k_ref, v_ref, conv_ref, pad_ref, y_ref):
    i = pl.program_id(0)
    tm = x_ref.shape[0]

    @pl.when(i == 0)
    def _():
        pad_ref[...] = jnp.zeros(pad_ref.shape, F32)

    n_pieces = (tm // CONV_CHUNK) * CONV_SLABS
    assert n_pieces <= 3 * (D_FF // FF_CHUNK) - 1
    order = [jnp.zeros((1, LANES), F32)]

    def conv_side_work(piece):
        if not 0 <= piece < n_pieces:
            return None
        order[0] = _conv_piece(piece, order[0], pad_ref, y_ref, cw_ref, cb_ref, cg_ref, cbeta_ref, conv_ref)
        return order[0]

    x = x_ref[...]
    h1 = x + 0.5 * _swiglu(_rms(x, n1_ref[...]).astype(BF16), wg_ref, wu_ref, wd_ref, conv_side_work)
    h1_ref[...] = h1

    u = jnp.dot(_rms(h1, nm_ref[...]).astype(BF16), win_ref[...], preferred_element_type=F32)
    q = u[:, 0:D_ATTN]
    k = u[:, D_ATTN:2 * D_ATTN]
    v_ref[...] = u[:, 2 * D_ATTN:3 * D_ATTN]
    ca = u[:, 3 * D_ATTN:3 * D_ATTN + D_CONV]
    cg = u[:, 3 * D_ATTN + D_CONV:]

    hm = hm_ref[...]
    q_ms = jnp.dot((q * q).astype(BF16), hm, preferred_element_type=F32)
    k_ms = jnp.dot((k * k).astype(BF16), hm, preferred_element_type=F32)
    q_ref[...] = q * lax.rsqrt(q_ms + EPS) * (qg_ref[...] * (HEAD_DIM ** -0.5))
    k_ref[...] = k * lax.rsqrt(k_ms + EPS) * kg_ref[...]

    same_seq = (i % tiles_per_seq) != 0
    pad_ref[0:HALO, :] = jnp.where(same_seq, pad_ref[tm:tm + HALO, :], 0.0)
    pad_ref[HALO:, :] = ca * jax.nn.sigmoid(cg)


def _ffn_inproj(x2, seq, n1, wg, wu, wd, nm, win, qg, kg, hm, cw, cb, cg, cbeta):
    T = x2.shape[0]
    tm = TM_FFN
    n_tiles = T // tm
    cur = lambda w: pl.BlockSpec((tm, w), lambda i: (jnp.minimum(i, n_tiles - 1), 0))
    lag = lambda w: pl.BlockSpec((tm, w), lambda i: (jnp.maximum(i - 1, 0), 0))
    return pl.pallas_call(
        functools.partial(_ffn_inproj_kernel, seq // tm),
        grid=(n_tiles + 1,),
        in_specs=[cur(D_MODEL), _const_spec((1, D_MODEL)),
                  _const_spec((D_MODEL, D_FF)), _const_spec((D_MODEL, D_FF)), _const_spec((D_FF, D_MODEL)),
                  _const_spec((1, D_MODEL)), _const_spec((D_MODEL, D_IN)),
                  _const_spec((1, D_ATTN)), _const_spec((1, D_ATTN)), _const_spec((D_ATTN, D_ATTN)),
                  _const_spec((CONV_K, D_CONV)), _const_spec((1, D_CONV)), _const_spec((1, D_CONV)),
                  _const_spec((1, D_CONV))],
        out_specs=[cur(D_MODEL), cur(D_ATTN), cur(D_ATTN), cur(D_ATTN), lag(D_CONV)],
        out_shape=[jax.ShapeDtypeStruct((T, D_MODEL), F32)] + [jax.ShapeDtypeStruct((T, D_ATTN), F32)] * 3
                  + [jax.ShapeDtypeStruct((T, D_CONV), BF16)],
        scratch_shapes=[pltpu.VMEM((tm + HALO, D_CONV), F32), pltpu.VMEM((tm, D_CONV), F32)],
        compiler_params=pltpu.CompilerParams(dimension_semantics=("arbitrary",),
                                             vmem_limit_bytes=VMEM_LIMIT_BYTES),
        name="ffn_inproj_conv",
    )(x2, n1, wg, wu, wd, nm, win, qg, kg, hm, cw, cb, cg, cbeta)


def _rows(ref, start, dilation):
    if dilation == 1:
        return ref[pl.ds(start, BLK), :]
    return ref[pl.ds(start, BLK, stride=dilation), :]


def _store_rows(ref, start, dilation, val):
    if dilation == 1:
        ref[pl.ds(start, BLK), :] = val
    else:
        ref[pl.ds(start, BLK, stride=dilation), :] = val


def _fill_band_mask(mask_ref):
    qi = lax.broadcasted_iota(jnp.int32, (2 * BLK, 2 * BLK), 0) & (BLK - 1)
    ci = lax.broadcasted_iota(jnp.int32, (2 * BLK, 2 * BLK), 1)
    dist = BLK + qi - ci
    mask_ref[...] = jnp.where((dist >= 0) & (dist <= BLK), 0.0, NEG)


def _attn_chain(refs, start0, dilation, chain, has_prev):
    q_ref, k_ref, v_ref, o_ref, l_ref, mask_ref = refs
    span = BLK * dilation
    lane = lax.broadcasted_iota(jnp.int32, (BLK, PAIR), 1)
    first = lane < HEAD_DIM
    lo = -1 if has_prev else 0
    k_t = [_rows(k_ref, start0 + j * span, dilation).astype(BF16) for j in range(lo, chain)]
    v_t = [_rows(v_ref, start0 + j * span, dilation).astype(BF16) for j in range(lo, chain)]
    for j in range(chain):
        start = start0 + j * span
        q = _rows(q_ref, start, dilation)
        q2 = jnp.concatenate([jnp.where(first, q, 0.0), jnp.where(first, 0.0, q)], axis=0).astype(BF16)
        t = j - lo
        if t > 0:
            k_all = jnp.concatenate([k_t[t - 1], k_t[t]], axis=0)
            v_all = jnp.concatenate([v_t[t - 1], v_t[t]], axis=0)
            mask = mask_ref[...]
        else:
            k_all, v_all, mask = k_t[0], v_t[0], mask_ref[:, BLK:]
        v_aug = jnp.concatenate([v_all, jnp.ones(v_all.shape, BF16)], axis=1)

        s = lax.dot_general(q2, k_all, (((1,), (1,)), ((), ())), preferred_element_type=F32)
        s = s + mask
        m = jnp.max(s, axis=-1, keepdims=True)
        p = jnp.exp(s - m).astype(BF16)
        pv = jnp.dot(p, v_aug, preferred_element_type=F32)
        l = pv[:, PAIR:]
        o = pv[:, :PAIR] / l
        lse = m + jnp.log(l)
        _store_rows(o_ref, start, dilation, jnp.where(first, o[:BLK], o[BLK:]))
        _store_rows(l_ref, start, dilation, jnp.where(first, lse[:BLK], lse[BLK:]))


def _attn_pattern(refs, seq, dilation):
    span = BLK * dilation
    n_blocks = seq // span
    chain = min(n_blocks, ATTN_GROUP)
    for r in range(dilation):
        for c in range(n_blocks // chain):
            _attn_chain(refs, c * chain * span + r, dilation, chain, has_prev=c > 0)


def _attn_kernel(q_ref, k_ref, v_ref, out_ref, mask_ref, o1, l1, o4, l4, o16, l16):
    seq = q_ref.shape[0]
    _fill_band_mask(mask_ref)
    for dilation, o_ref, l_ref in zip(DILATIONS, (o1, o4, o16), (l1, l4, l16)):
        _attn_pattern((q_ref, k_ref, v_ref, o_ref, l_ref, mask_ref), seq, dilation)

    la, lb, lc = l1[...], l4[...], l16[...]
    m = jnp.maximum(jnp.maximum(la, lb), lc)
    wa, wb, wc = jnp.exp(la - m), jnp.exp(lb - m), jnp.exp(lc - m)
    out = (wa * o1[...] + wb * o4[...] + wc * o16[...]) / (wa + wb + wc)
    out_ref[...] = out.astype(out_ref.dtype)


def _dilated_attn(q, k, v):
    B, S, _ = q.shape
    spec = pl.BlockSpec((None, S, PAIR), lambda b, h: (b, 0, h))
    return pl.pallas_call(
        _attn_kernel,
        grid=(B, N_PAIRS),
        in_specs=[spec, spec, spec],
        out_specs=spec,
        out_shape=jax.ShapeDtypeStruct((B, S, D_ATTN), BF16),
        scratch_shapes=[pltpu.VMEM((2 * BLK, 2 * BLK), F32)] + [pltpu.VMEM((S, PAIR), F32)] * 6,
        compiler_params=pltpu.CompilerParams(dimension_semantics=("arbitrary", "arbitrary"),
                                             vmem_limit_bytes=VMEM_LIMIT_BYTES),
        name="dilated_attn",
    )(q, k, v)


def _outproj_ffn_kernel(h1_ref, attn_ref, conv_ref, woa_ref, woc_ref, n2_ref, wg_ref, wu_ref, wd_ref,
                        out_ref):
    mix = (jnp.dot(attn_ref[...], woa_ref[...], preferred_element_type=F32)
           + jnp.dot(conv_ref[...], woc_ref[...], preferred_element_type=F32))
    h2 = h1_ref[...] + mix
    out_ref[...] = h2 + 0.5 * _swiglu(_rms(h2, n2_ref[...]).astype(BF16), wg_ref, wu_ref, wd_ref)


def _outproj_ffn(h1, attn, conv, woa, woc, n2, wg, wu, wd):
    T = h1.shape[0]
    tm = TM_FFN
    row = lambda w: pl.BlockSpec((tm, w), lambda i: (i, 0))
    return pl.pallas_call(
        _outproj_ffn_kernel,
        grid=(T // tm,),
        in_specs=[row(D_MODEL), row(D_ATTN), row(D_CONV),
                  _const_spec((D_ATTN, D_MODEL)), _const_spec((D_CONV, D_MODEL)), _const_spec((1, D_MODEL)),
                  _const_spec((D_MODEL, D_FF)), _const_spec((D_MODEL, D_FF)), _const_spec((D_FF, D_MODEL))],
        out_specs=row(D_MODEL),
        out_shape=jax.ShapeDtypeStruct((T, D_MODEL), F32),
        compiler_params=pltpu.CompilerParams(dimension_semantics=("arbitrary",),
                                             vmem_limit_bytes=VMEM_LIMIT_BYTES),
        name="outproj_ffn",
    )(h1, attn, conv, woa, woc, n2, wg, wu, wd)


def _layer(h, ffn1_norm, ffn1_w_gate, ffn1_w_up, ffn1_w_down, mix_norm, w_in, q_norm, k_norm, conv_w,
           conv_b, conv_ln_g, conv_ln_b, w_out, ffn2_norm, ffn2_w_gate, ffn2_w_up, ffn2_w_down):
    B, S, D = h.shape
    T = B * S
    row = lambda a: a.reshape(1, -1).astype(F32)
    head_id = jnp.arange(D_ATTN) // HEAD_DIM
    head_mean = jnp.where(head_id[:, None] == head_id[None, :], 1.0 / HEAD_DIM, 0.0).astype(BF16)

    h1, q, k, v, conv = _ffn_inproj(
        h.reshape(T, D), S, row(ffn1_norm), ffn1_w_gate.astype(BF16), ffn1_w_up.astype(BF16),
        ffn1_w_down.astype(BF16), row(mix_norm), w_in.astype(BF16),
        row(jnp.tile(q_norm, HEADS)), row(jnp.tile(k_norm, HEADS)), head_mean,
        conv_w.astype(F32), row(conv_b), row(conv_ln_g), row(conv_ln_b))

    attn = _dilated_attn(q.reshape(B, S, D_ATTN), k.reshape(B, S, D_ATTN), v.reshape(B, S, D_ATTN))

    w_out16 = w_out.astype(BF16)
    out = _outproj_ffn(h1, attn.reshape(T, D_ATTN), conv, w_out16[:D_ATTN], w_out16[D_ATTN:],
                       row(ffn2_norm), ffn2_w_gate.astype(BF16), ffn2_w_up.astype(BF16),
                       ffn2_w_down.astype(BF16))
    return out.reshape(B, S, D)


def kernel(x, ffn1_norm, ffn1_w_gate, ffn1_w_up, ffn1_w_down, mix_norm, w_in, q_norm, k_norm, conv_w, conv_b,
           conv_ln_g, conv_ln_b, w_out, ffn2_norm, ffn2_w_gate, ffn2_w_up, ffn2_w_down):
    assert x.shape[1] % (BLK * max(DILATIONS)) == 0 and x.shape[1] % TM_FFN == 0 and x.shape[2] == D_MODEL
    h = x
    for l in range(ffn1_norm.shape[0]):
        h = _layer(h, ffn1_norm[l], ffn1_w_gate[l], ffn1_w_up[l], ffn1_w_down[l], mix_norm[l], w_in[l],
                   q_norm[l], k_norm[l], conv_w[l], conv_b[l], conv_ln_g[l], conv_ln_b[l], w_out[l],
                   ffn2_norm[l], ffn2_w_gate[l], ffn2_w_up[l], ffn2_w_down[l])
    return h
```

```python
import functools

import jax
import jax.numpy as jnp
from jax import lax
from jax.experimental import pallas as pl
from jax.experimental.pallas import tpu as pltpu

F32 = jnp.float32
BF16 = jnp.bfloat16

D_MODEL = 1024
D_FF = 2816
HEADS = 8
HEAD_DIM = 64
D_ATTN = HEADS * HEAD_DIM
D_CONV = 512
D_IN = 3 * D_ATTN + 2 * D_CONV
CONV_K = 31
BLK = 128
DILATIONS = (1, 4, 16)
EPS = 1e-6
NEG = -1e30

LANES = 128
SUBLANES = 8
PAIR = 2 * HEAD_DIM
N_PAIRS = D_ATTN // PAIR
HALO = 32

VMEM_LIMIT_BYTES = 56 * 1024 * 1024

TM_FFN = 512
FF_CHUNK = 256
CONV_CHUNK = 64
CONV_SLABS = D_CONV // LANES
ATTN_GROUP = 4
ATTN_PLANES = 4
ATTN_IN_FLIGHT = 6


def _const_spec(shape):
    return pl.BlockSpec(shape, lambda *_: (0,) * len(shape), pipeline_mode=pl.Buffered(1))


def _rms(x, g):
    ms = jnp.mean(x * x, axis=-1, keepdims=True)
    return x * lax.rsqrt(ms + EPS) * g


def _swiglu(xn, wg_ref, wu_ref, wd_ref, side_work=None):
    def after(piece, operand):
        order = side_work(piece) if side_work is not None else None
        if order is None:
            return operand
        order = jnp.concatenate([order] * (operand.shape[1] // LANES), axis=1)
        return operand + order.astype(operand.dtype)

    acc = None
    for c in range(D_FF // FF_CHUNK):
        cols = slice(c * FF_CHUNK, (c + 1) * FF_CHUNK)
        g = jnp.dot(after(3 * c - 1, xn), wg_ref[:, cols], preferred_element_type=F32)
        u = jnp.dot(after(3 * c, xn), wu_ref[:, cols], preferred_element_type=F32)
        a = after(3 * c + 1, g * jax.nn.sigmoid(g) * u)
        part = jnp.dot(a.astype(BF16), wd_ref[cols, :], preferred_element_type=F32)
        acc = part if acc is None else acc + part
    return acc


def _zero_after(x):
    bits = pltpu.bitcast(x, jnp.uint32)
    fold = bits[0:SUBLANES, :]
    for r in range(SUBLANES, x.shape[0], SUBLANES):
        fold = fold | bits[r:r + SUBLANES, :]
    zero = lax.shift_right_logical(lax.shift_right_logical(fold, jnp.uint32(16)), jnp.uint32(16))
    return pltpu.bitcast(zero, F32)[0:1, :]


def _conv_taps(pad_ref, w_ref, row0, lanes, order):
    first = HALO - (CONV_K - 1)
    y = None
    for b in range(SUBLANES):
        rows = CONV_CHUNK + (SUBLANES if b else 0)
        u = None
        for a in range(HALO // SUBLANES + 1):
            j = SUBLANES * a + b - first
            if 0 <= j < CONV_K:
                lo = row0 + SUBLANES * a
                term = pad_ref[lo:lo + rows, lanes] * (w_ref[j:j + 1, lanes] + order)
                u = term if u is None else u + term
        u = u[b:b + CONV_CHUNK, :] if b else u
        y = u if y is None else y + u
    return y


def _conv_piece(piece, order, pad_ref, y_ref, w_ref, b_ref, g_ref, beta_ref, out_ref):
    c, s = divmod(piece, CONV_SLABS)
    rows = slice(c * CONV_CHUNK, (c + 1) * CONV_CHUNK)
    lanes = slice(s * LANES, (s + 1) * LANES)
    y = _conv_taps(pad_ref, w_ref, c * CONV_CHUNK, lanes, order)
    y_ref[rows, lanes] = y + b_ref[:, lanes]
    if s == CONV_SLABS - 1:
        yr = y_ref[rows, :]
        mu = jnp.mean(yr, axis=-1, keepdims=True)
        yc = yr - mu
        var = jnp.mean(yc * yc, axis=-1, keepdims=True)
        z = yc * lax.rsqrt(var + EPS) * g_ref[...] + beta_ref[...]
        out_ref[rows, :] = (z * jax.nn.sigmoid(z)).astype(out_ref.dtype)
    return _zero_after(y)


def _ffn_inproj_kernel(tiles_per_seq, x_ref, n1_ref, wg_ref, wu_ref, wd_ref, nm_ref, win_ref, qg_ref,
                       kg_ref, hm_ref, cw_ref, cb_ref, cg_ref, cbeta_ref,
                       h1_ref, q_ref, k_ref, v_ref, conv_ref, pad_ref, y_ref):
    i = pl.program_id(0)
    tm = x_ref.shape[0]

    @pl.when(i == 0)
    def _():
        pad_ref[...] = jnp.zeros(pad_ref.shape, F32)

    n_pieces = (tm // CONV_CHUNK) * CONV_SLABS
    assert n_pieces <= 3 * (D_FF // FF_CHUNK) - 1
    order = [jnp.zeros((1, LANES), F32)]

    def conv_side_work(piece):
        if not 0 <= piece < n_pieces:
            return None
        order[0] = _conv_piece(piece, order[0], pad_ref, y_ref, cw_ref, cb_ref, cg_ref, cbeta_ref, conv_ref)
        return order[0]

    x = x_ref[...]
    h1 = x + 0.5 * _swiglu(_rms(x, n1_ref[...]).astype(BF16), wg_ref, wu_ref, wd_ref, conv_side_work)
    h1_ref[...] = h1

    u = jnp.dot(_rms(h1, nm_ref[...]).astype(BF16), win_ref[...], preferred_element_type=F32)
    q = u[:, 0:D_ATTN]
    k = u[:, D_ATTN:2 * D_ATTN]
    v_ref[...] = u[:, 2 * D_ATTN:3 * D_ATTN]
    ca = u[:, 3 * D_ATTN:3 * D_ATTN + D_CONV]
    cg = u[:, 3 * D_ATTN + D_CONV:]

    hm = hm_ref[...]
    q_ms = jnp.dot((q * q).astype(BF16), hm, preferred_element_type=F32)
    k_ms = jnp.dot((k * k).astype(BF16), hm, preferred_element_type=F32)
    q_ref[...] = q * lax.rsqrt(q_ms + EPS) * (qg_ref[...] * (HEAD_DIM ** -0.5))
    k_ref[...] = k * lax.rsqrt(k_ms + EPS) * kg_ref[...]

    same_seq = (i % tiles_per_seq) != 0
    pad_ref[0:HALO, :] = jnp.where(same_seq, pad_ref[tm:tm + HALO, :], 0.0)
    pad_ref[HALO:, :] = ca * jax.nn.sigmoid(cg)


def _ffn_inproj(x2, seq, n1, wg, wu, wd, nm, win, qg, kg, hm, cw, cb, cg, cbeta):
    T = x2.shape[0]
    tm = TM_FFN
    n_tiles = T // tm
    cur = lambda w: pl.BlockSpec((tm, w), lambda i: (jnp.minimum(i, n_tiles - 1), 0))
    lag = lambda w: pl.BlockSpec((tm, w), lambda i: (jnp.maximum(i - 1, 0), 0))
    return pl.pallas_call(
        functools.partial(_ffn_inproj_kernel, seq // tm),
        grid=(n_tiles + 1,),
        in_specs=[cur(D_MODEL), _const_spec((1, D_MODEL)),
                  _const_spec((D_MODEL, D_FF)), _const_spec((D_MODEL, D_FF)), _const_spec((D_FF, D_MODEL)),
                  _const_spec((1, D_MODEL)), _const_spec((D_MODEL, D_IN)),
                  _const_spec((1, D_ATTN)), _const_spec((1, D_ATTN)), _const_spec((D_ATTN, D_ATTN)),
                  _const_spec((CONV_K, D_CONV)), _const_spec((1, D_CONV)), _const_spec((1, D_CONV)),
                  _const_spec((1, D_CONV))],
        out_specs=[cur(D_MODEL), cur(D_ATTN), cur(D_ATTN), cur(D_ATTN), lag(D_CONV)],
        out_shape=[jax.ShapeDtypeStruct((T, D_MODEL), F32)] + [jax.ShapeDtypeStruct((T, D_ATTN), F32)] * 3
                  + [jax.ShapeDtypeStruct((T, D_CONV), BF16)],
        scratch_shapes=[pltpu.VMEM((tm + HALO, D_CONV), F32), pltpu.VMEM((tm, D_CONV), F32)],
        compiler_params=pltpu.CompilerParams(dimension_semantics=("arbitrary",),
                                             vmem_limit_bytes=VMEM_LIMIT_BYTES),
        name="ffn_inproj_conv",
    )(x2, n1, wg, wu, wd, nm, win, qg, kg, hm, cw, cb, cg, cbeta)


def _rows(ref, start, stride):
    if stride == 1:
        return ref[pl.ds(start, BLK), :]
    return ref[pl.ds(start, BLK, stride=stride), :]


def _store_rows(ref, start, stride, val):
    if stride == 1:
        ref[pl.ds(start, BLK), :] = val
    else:
        ref[pl.ds(start, BLK, stride=stride), :] = val


def _fill_band_mask(mask_ref):
    qi = lax.broadcasted_iota(jnp.int32, (2 * BLK, 2 * BLK), 0) & (BLK - 1)
    ci = lax.broadcasted_iota(jnp.int32, (2 * BLK, 2 * BLK), 1)
    dist = BLK + qi - ci
    mask_ref[...] = jnp.where((dist >= 0) & (dist <= BLK), 0.0, NEG)


def _attn_chain(src, dst, mask_ref, chain, has_prev, order):
    q_ref, k_ref, v_ref, src0, src_stride, src_step = src
    a_ref, m_ref, l_ref, dst0, dst_stride, dst_step = dst
    lane = lax.broadcasted_iota(jnp.int32, (BLK, PAIR), 1)
    first = lane < HEAD_DIM
    lo = -1 if has_prev else 0
    k_t = [_rows(k_ref, src0 + j * src_step, src_stride).astype(BF16) for j in range(lo, chain)]
    v_t = [_rows(v_ref, src0 + j * src_step, src_stride).astype(BF16) for j in range(lo, chain)]
    accs = []
    for j in range(chain):
        q = _rows(q_ref, src0 + j * src_step, src_stride)
        if order is not None:
            q = q + order
        q2 = jnp.concatenate([jnp.where(first, q, 0.0), jnp.where(first, 0.0, q)], axis=0).astype(BF16)
        t = j - lo
        if t > 0:
            k_all = jnp.concatenate([k_t[t - 1], k_t[t]], axis=0)
            v_all = jnp.concatenate([v_t[t - 1], v_t[t]], axis=0)
            mask = mask_ref[...]
        else:
            k_all, v_all, mask = k_t[0], v_t[0], mask_ref[:, BLK:]
        v_aug = jnp.concatenate([v_all, jnp.ones(v_all.shape, BF16)], axis=1)

        s = lax.dot_general(q2, k_all, (((1,), (1,)), ((), ())), preferred_element_type=F32)
        s = s + mask
        m = jnp.max(s, axis=-1, keepdims=True)
        p = jnp.exp(s - m).astype(BF16)
        pv = jnp.dot(p, v_aug, preferred_element_type=F32)
        row = dst0 + j * dst_step
        acc = jnp.where(first, pv[:BLK, :PAIR], pv[BLK:, :PAIR])
        _store_rows(a_ref, row, dst_stride, acc)
        _store_rows(l_ref, row, dst_stride, jnp.where(first, pv[:BLK, PAIR:], pv[BLK:, PAIR:]))
        _store_rows(m_ref, row, dst_stride, jnp.where(first, m[:BLK], m[BLK:]))
        accs.append(acc)
    return _zero_after(jnp.concatenate(accs, axis=0))


def _attn_kernel(q_ref, k_ref, v_ref, out_ref, mask_ref, qp_ref, kp_ref, vp_ref, *stats):
    seq = q_ref.shape[0]
    _fill_band_mask(mask_ref)
    for src_ref, plane_ref in ((q_ref, qp_ref), (k_ref, kp_ref), (v_ref, vp_ref)):
        for p in range(ATTN_PLANES):
            for c in range(0, seq // ATTN_PLANES, BLK):
                plane_ref[p, c:c + BLK, :] = src_ref[pl.ds(ATTN_PLANES * c + p, BLK, stride=ATTN_PLANES), :]

    def chains_of(d, plane):
        dilation = DILATIONS[d]
        dst_refs = stats[3 * d:3 * d + 3]
        span = BLK * dilation
        n_blocks = seq // span
        chain = min(n_blocks, ATTN_GROUP)
        chains = []
        for r in range(dilation):
            if plane is not None and r % ATTN_PLANES != plane:
                continue
            for c in range(n_blocks // chain):
                start = c * chain * span + r
                if plane is None:
                    src = (q_ref, k_ref, v_ref, start, dilation, span)
                    dst = dst_refs + (start, dilation, span)
                else:
                    where = (start // ATTN_PLANES, dilation // ATTN_PLANES, span // ATTN_PLANES)
                    src = (qp_ref.at[plane], kp_ref.at[plane], vp_ref.at[plane]) + where
                    dst = tuple(ref.at[plane] for ref in dst_refs) + where
                chains.append(functools.partial(_attn_chain, src, dst, mask_ref, chain, c > 0))
        per_piece = ATTN_GROUP // chain

        def run(order, group):
            return functools.reduce(jnp.add, [one(order) for one in group])
        return [functools.partial(run, group=chains[g:g + per_piece]) for g in range(0, len(chains), per_piece)]

    def mix_piece(order, plane):
        per_plane = seq // ATTN_PLANES

        def rows_of(ref, dilation):
            if dilation % ATTN_PLANES:
                return ref[pl.ds(plane, per_plane, stride=ATTN_PLANES), :]
            return ref[plane]
        accs, ms, sums = ([rows_of(stats[3 * d + i], dil) for d, dil in enumerate(DILATIONS)] for i in range(3))
        if order is not None:
            ms[0] = ms[0] + order
        top = functools.reduce(jnp.maximum, ms)
        es = [jnp.exp(m - top) for m in ms]
        num = functools.reduce(jnp.add, [e * a for e, a in zip(es, accs)])
        den = functools.reduce(jnp.add, [e * l for e, l in zip(es, sums)])
        out = num / den
        out_ref[pl.ds(plane, per_plane, stride=ATTN_PLANES), :] = out
        return _zero_after(out)

    pieces = [piece for d, dil in enumerate(DILATIONS) if dil % ATTN_PLANES for piece in chains_of(d, None)]
    for plane in range(ATTN_PLANES):
        for d, dil in enumerate(DILATIONS):
            if dil % ATTN_PLANES == 0:
                pieces += chains_of(d, plane)
        pieces.append(functools.partial(mix_piece, plane=plane))
    orders = []
    for piece in pieces:
        orders.append(piece(orders[-ATTN_IN_FLIGHT] if len(orders) >= ATTN_IN_FLIGHT else None))


def _dilated_attn(q, k, v):
    B, S, _ = q.shape
    spec = pl.BlockSpec((None, S, PAIR), lambda b, h: (b, 0, h))
    planes = pltpu.VMEM((ATTN_PLANES, S // ATTN_PLANES, PAIR), F32)
    return pl.pallas_call(
        _attn_kernel,
        grid=(B, N_PAIRS),
        in_specs=[spec, spec, spec],
        out_specs=spec,
        out_shape=jax.ShapeDtypeStruct((B, S, D_ATTN), F32),
        scratch_shapes=[pltpu.VMEM((2 * BLK, 2 * BLK), F32)] + [planes] * 3
                       + [pltpu.VMEM((S, PAIR), F32) if d % ATTN_PLANES else planes
                          for d in DILATIONS for _ in range(3)],
        compiler_params=pltpu.CompilerParams(dimension_semantics=("arbitrary", "arbitrary"),
                                             vmem_limit_bytes=VMEM_LIMIT_BYTES),
        name="dilated_attn",
    )(q, k, v)


def _outproj_ffn_kernel(h1_ref, attn_ref, conv_ref, woa_ref, woc_ref, n2_ref, wg_ref, wu_ref, wd_ref,
                        out_ref):
    mix = (jnp.dot(attn_ref[...].astype(BF16), woa_ref[...], preferred_element_type=F32)
           + jnp.dot(conv_ref[...], woc_ref[...], preferred_element_type=F32))
    h2 = h1_ref[...] + mix
    out_ref[...] = h2 + 0.5 * _swiglu(_rms(h2, n2_ref[...]).astype(BF16), wg_ref, wu_ref, wd_ref)


def _outproj_ffn(h1, attn, conv, woa, woc, n2, wg, wu, wd):
    T = h1.shape[0]
    tm = TM_FFN
    row = lambda w: pl.BlockSpec((tm, w), lambda i: (i, 0))
    return pl.pallas_call(
        _outproj_ffn_kernel,
        grid=(T // tm,),
        in_specs=[row(D_MODEL), row(D_ATTN), row(D_CONV),
                  _const_spec((D_ATTN, D_MODEL)), _const_spec((D_CONV, D_MODEL)), _const_spec((1, D_MODEL)),
                  _const_spec((D_MODEL, D_FF)), _const_spec((D_MODEL, D_FF)), _const_spec((D_FF, D_MODEL))],
        out_specs=row(D_MODEL),
        out_shape=jax.ShapeDtypeStruct((T, D_MODEL), F32),
        compiler_params=pltpu.CompilerParams(dimension_semantics=("arbitrary",),
                                             vmem_limit_bytes=VMEM_LIMIT_BYTES),
        name="outproj_ffn",
    )(h1, attn, conv, woa, woc, n2, wg, wu, wd)


def _layer(h, ffn1_norm, ffn1_w_gate, ffn1_w_up, ffn1_w_down, mix_norm, w_in, q_norm, k_norm, conv_w,
           conv_b, conv_ln_g, conv_ln_b, w_out, ffn2_norm, ffn2_w_gate, ffn2_w_up, ffn2_w_down):
    B, S, D = h.shape
    T = B * S
    row = lambda a: a.reshape(1, -1).astype(F32)
    head_id = jnp.arange(D_ATTN) // HEAD_DIM
    head_mean = jnp.where(head_id[:, None] == head_id[None, :], 1.0 / HEAD_DIM, 0.0).astype(BF16)

    h1, q, k, v, conv = _ffn_inproj(
        h.reshape(T, D), S, row(ffn1_norm), ffn1_w_gate.astype(BF16), ffn1_w_up.astype(BF16),
        ffn1_w_down.astype(BF16), row(mix_norm), w_in.astype(BF16),
        row(jnp.tile(q_norm, HEADS)), row(jnp.tile(k_norm, HEADS)), head_mean,
        conv_w.astype(F32), row(conv_b), row(conv_ln_g), row(conv_ln_b))

    attn = _dilated_attn(q.reshape(B, S, D_ATTN), k.reshape(B, S, D_ATTN), v.reshape(B, S, D_ATTN))

    w_out16 = w_out.astype(BF16)
    out = _outproj_ffn(h1, attn.reshape(T, D_ATTN), conv, w_out16[:D_ATTN], w_out16[D_ATTN:],
                       row(ffn2_norm), ffn2_w_gate.astype(BF16), ffn2_w_up.astype(BF16),
                       ffn2_w_down.astype(BF16))
    return out.reshape(B, S, D)


def kernel(x, ffn1_norm, ffn1_w_gate, ffn1_w_up, ffn1_w_down, mix_norm, w_in, q_norm, k_norm, conv_w, conv_b,
           conv_ln_g, conv_ln_b, w_out, ffn2_norm, ffn2_w_gate, ffn2_w_up, ffn2_w_down):
    assert x.shape[1] % (BLK * max(DILATIONS)) == 0 and x.shape[1] % TM_FFN == 0 and x.shape[2] == D_MODEL
    h = x
    for l in range(ffn1_norm.shape[0]):
        h = _layer(h, ffn1_norm[l], ffn1_w_gate[l], ffn1_w_up[l], ffn1_w_down[l], mix_norm[l], w_in[l],
                   q_norm[l], k_norm[l], conv_w[l], conv_b[l], conv_ln_g[l], conv_ln_b[l], w_out[l],
                   ffn2_norm[l], ffn2_w_gate[l], ffn2_w_up[l], ffn2_w_down[l])
    return h
```

```python
import functools

import jax
import jax.numpy as jnp
from jax import lax
from jax.experimental import pallas as pl
from jax.experimental.pallas import tpu as pltpu

F32 = jnp.float32
BF16 = jnp.bfloat16

D_MODEL = 1024
D_FF = 2816
HEADS = 8
HEAD_DIM = 64
D_ATTN = HEADS * HEAD_DIM
D_CONV = 512
D_IN = 3 * D_ATTN + 2 * D_CONV
CONV_K = 31
BLK = 128
DILATIONS = (1, 4, 16)
EPS = 1e-6
NEG = -1e30

LANES = 128
SUBLANES = 8
PAIR = 2 * HEAD_DIM
N_PAIRS = D_ATTN // PAIR
HALO = 32

VMEM_LIMIT_BYTES = 56 * 1024 * 1024

TM_FFN = 512
TM_OUT = 1024
FF_CHUNK = 256
CONV_CHUNK = 64
CONV_SLABS = D_CONV // LANES
ATTN_GROUP = 4
ATTN_PLANES = 4
ATTN_IN_FLIGHT = 6


def _const_spec(shape):
    return pl.BlockSpec(shape, lambda *_: (0,) * len(shape), pipeline_mode=pl.Buffered(1))


def _rms(x, g):
    ms = jnp.mean(x * x, axis=-1, keepdims=True)
    return x * lax.rsqrt(ms + EPS) * g


def _swiglu(xn, wg_ref, wu_ref, wd_ref, side_work=None):
    def after(piece, operand):
        order = side_work(piece) if side_work is not None else None
        if order is None:
            return operand
        head = 2 * SUBLANES
        order = jnp.concatenate([order] * (operand.shape[1] // LANES), axis=1).astype(operand.dtype)
        return jnp.concatenate([operand[:head] + order, operand[head:]], axis=0)

    acc = None
    for c in range(D_FF // FF_CHUNK):
        cols = slice(c * FF_CHUNK, (c + 1) * FF_CHUNK)
        g = jnp.dot(after(3 * c - 1, xn), wg_ref[:, cols], preferred_element_type=F32)
        u = jnp.dot(after(3 * c, xn), wu_ref[:, cols], preferred_element_type=F32)
        a = after(3 * c + 1, g * jax.nn.sigmoid(g) * u)
        part = jnp.dot(a.astype(BF16), wd_ref[cols, :], preferred_element_type=F32)
        acc = part if acc is None else acc + part
    return acc


def _zero_after(x):
    bits = pltpu.bitcast(x, jnp.uint32)
    fold = bits[0:SUBLANES, :]
    for r in range(SUBLANES, x.shape[0], SUBLANES):
        fold = fold | bits[r:r + SUBLANES, :]
    zero = lax.shift_right_logical(lax.shift_right_logical(fold, jnp.uint32(16)), jnp.uint32(16))
    return pltpu.bitcast(zero, F32)[0:1, :]


def _conv_taps(pad_ref, w_ref, row0, lanes, order):
    first = HALO - (CONV_K - 1)
    y = None
    for b in range(SUBLANES):
        rows = CONV_CHUNK + (SUBLANES if b else 0)
        u = None
        for a in range(HALO // SUBLANES + 1):
            j = SUBLANES * a + b - first
            if 0 <= j < CONV_K:
                lo = row0 + SUBLANES * a
                term = pad_ref[lo:lo + rows, lanes] * (w_ref[j:j + 1, lanes] + order)
                u = term if u is None else u + term
        u = u[b:b + CONV_CHUNK, :] if b else u
        y = u if y is None else y + u
    return y


def _conv_piece(piece, order, pad_ref, y_ref, w_ref, b_ref, g_ref, beta_ref, out_ref):
    c, s = divmod(piece, CONV_SLABS)
    rows = slice(c * CONV_CHUNK, (c + 1) * CONV_CHUNK)
    lanes = slice(s * LANES, (s + 1) * LANES)
    y = _conv_taps(pad_ref, w_ref, c * CONV_CHUNK, lanes, order)
    y_ref[rows, lanes] = y + b_ref[:, lanes]
    if s == CONV_SLABS - 1:
        yr = y_ref[rows, :]
        mu = jnp.mean(yr, axis=-1, keepdims=True)
        yc = yr - mu
        var = jnp.mean(yc * yc, axis=-1, keepdims=True)
        z = yc * lax.rsqrt(var + EPS) * g_ref[...] + beta_ref[...]
        out_ref[rows, :] = (z * jax.nn.sigmoid(z)).astype(out_ref.dtype)
    return _zero_after(y)


def _ffn_inproj_kernel(tiles_per_seq, x_ref, n1_ref, wg_ref, wu_ref, wd_ref, nm_ref, win_ref, qg_ref,
                       kg_ref, hm_ref, cw_ref, cb_ref, cg_ref, cbeta_ref,
                       h1_ref, q_ref, k_ref, v_ref, conv_ref, pad_ref, y_ref):
    i = pl.program_id(0)
    tm = x_ref.shape[0]

    @pl.when(i == 0)
    def _():
        pad_ref[...] = jnp.zeros(pad_ref.shape, F32)

    n_pieces = (tm // CONV_CHUNK) * CONV_SLABS
    assert n_pieces <= 3 * (D_FF // FF_CHUNK) - 1
    order = [jnp.zeros((1, LANES), F32)]

    def conv_side_work(piece):
        if not 0 <= piece < n_pieces:
            return None
        order[0] = _conv_piece(piece, order[0], pad_ref, y_ref, cw_ref, cb_ref, cg_ref, cbeta_ref, conv_ref)
        return order[0]

    x = x_ref[...]
    h1 = x + 0.5 * _swiglu(_rms(x, n1_ref[...]).astype(BF16), wg_ref, wu_ref, wd_ref, conv_side_work)
    h1_ref[...] = h1

    u = jnp.dot(_rms(h1, nm_ref[...]).astype(BF16), win_ref[...], preferred_element_type=F32)
    q = u[:, 0:D_ATTN]
    k = u[:, D_ATTN:2 * D_ATTN]
    v_ref[...] = u[:, 2 * D_ATTN:3 * D_ATTN]
    ca = u[:, 3 * D_ATTN:3 * D_ATTN + D_CONV]
    cg = u[:, 3 * D_ATTN + D_CONV:]

    hm = hm_ref[...]
    q_ms = jnp.dot((q * q).astype(BF16), hm, preferred_element_type=F32)
    k_ms = jnp.dot((k * k).astype(BF16), hm, preferred_element_type=F32)
    q_ref[...] = q * lax.rsqrt(q_ms + EPS) * (qg_ref[...] * (HEAD_DIM ** -0.5))
    k_ref[...] = k * lax.rsqrt(k_ms + EPS) * kg_ref[...]

    same_seq = (i % tiles_per_seq) != 0
    pad_ref[0:HALO, :] = jnp.where(same_seq, pad_ref[tm:tm + HALO, :], 0.0)
    pad_ref[HALO:, :] = ca * jax.nn.sigmoid(cg)


def _ffn_inproj(x2, seq, n1, wg, wu, wd, nm, win, qg, kg, hm, cw, cb, cg, cbeta):
    T = x2.shape[0]
    tm = TM_FFN
    n_tiles = T // tm
    cur = lambda w: pl.BlockSpec((tm, w), lambda i: (jnp.minimum(i, n_tiles - 1), 0))
    lag = lambda w: pl.BlockSpec((tm, w), lambda i: (jnp.maximum(i - 1, 0), 0))
    return pl.pallas_call(
        functools.partial(_ffn_inproj_kernel, seq // tm),
        grid=(n_tiles + 1,),
        in_specs=[cur(D_MODEL), _const_spec((1, D_MODEL)),
                  _const_spec((D_MODEL, D_FF)), _const_spec((D_MODEL, D_FF)), _const_spec((D_FF, D_MODEL)),
                  _const_spec((1, D_MODEL)), _const_spec((D_MODEL, D_IN)),
                  _const_spec((1, D_ATTN)), _const_spec((1, D_ATTN)), _const_spec((D_ATTN, D_ATTN)),
                  _const_spec((CONV_K, D_CONV)), _const_spec((1, D_CONV)), _const_spec((1, D_CONV)),
                  _const_spec((1, D_CONV))],
        out_specs=[cur(D_MODEL), cur(D_ATTN), cur(D_ATTN), cur(D_ATTN), lag(D_CONV)],
        out_shape=[jax.ShapeDtypeStruct((T, D_MODEL), F32)] + [jax.ShapeDtypeStruct((T, D_ATTN), F32)] * 3
                  + [jax.ShapeDtypeStruct((T, D_CONV), BF16)],
        scratch_shapes=[pltpu.VMEM((tm + HALO, D_CONV), F32), pltpu.VMEM((tm, D_CONV), F32)],
        compiler_params=pltpu.CompilerParams(dimension_semantics=("arbitrary",),
                                             vmem_limit_bytes=VMEM_LIMIT_BYTES),
        name="ffn_inproj_conv",
    )(x2, n1, wg, wu, wd, nm, win, qg, kg, hm, cw, cb, cg, cbeta)


def _rows(ref, start, stride):
    if stride == 1:
        return ref[pl.ds(start, BLK), :]
    return ref[pl.ds(start, BLK, stride=stride), :]


def _store_rows(ref, start, stride, val):
    if stride == 1:
        ref[pl.ds(start, BLK), :] = val
    else:
        ref[pl.ds(start, BLK, stride=stride), :] = val


def _fill_band_mask(mask_ref):
    qi = lax.broadcasted_iota(jnp.int32, (2 * BLK, 2 * BLK), 0) & (BLK - 1)
    ci = lax.broadcasted_iota(jnp.int32, (2 * BLK, 2 * BLK), 1)
    dist = BLK + qi - ci
    mask_ref[...] = jnp.where((dist >= 0) & (dist <= BLK), 0.0, NEG)


def _attn_chain(src, dst, mask_ref, chain, has_prev, order):
    q_ref, k_ref, v_ref, src0, src_stride, src_step = src
    a_ref, m_ref, l_ref, dst0, dst_stride, dst_step = dst
    lane = lax.broadcasted_iota(jnp.int32, (BLK, PAIR), 1)
    first = lane < HEAD_DIM
    lo = -1 if has_prev else 0
    k_t = [_rows(k_ref, src0 + j * src_step, src_stride).astype(BF16) for j in range(lo, chain)]
    v_t = [_rows(v_ref, src0 + j * src_step, src_stride).astype(BF16) for j in range(lo, chain)]
    accs = []
    for j in range(chain):
        q = _rows(q_ref, src0 + j * src_step, src_stride)
        if order is not None:
            q = q + order
        q2 = jnp.concatenate([jnp.where(first, q, 0.0), jnp.where(first, 0.0, q)], axis=0).astype(BF16)
        t = j - lo
        if t > 0:
            k_all = jnp.concatenate([k_t[t - 1], k_t[t]], axis=0)
            v_all = jnp.concatenate([v_t[t - 1], v_t[t]], axis=0)
            mask = mask_ref[...]
        else:
            k_all, v_all, mask = k_t[0], v_t[0], mask_ref[:, BLK:]
        v_aug = jnp.concatenate([v_all, jnp.ones(v_all.shape, BF16)], axis=1)

        s = lax.dot_general(q2, k_all, (((1,), (1,)), ((), ())), preferred_element_type=F32)
        s = s + mask
        m = jnp.max(s, axis=-1, keepdims=True)
        p = jnp.exp(s - m).astype(BF16)
        pv = jnp.dot(p, v_aug, preferred_element_type=F32)
        row = dst0 + j * dst_step
        acc = jnp.where(first, pv[:BLK, :PAIR], pv[BLK:, :PAIR])
        _store_rows(a_ref, row, dst_stride, acc)
        _store_rows(l_ref, row, dst_stride, jnp.where(first, pv[:BLK, PAIR:], pv[BLK:, PAIR:]))
        _store_rows(m_ref, row, dst_stride, jnp.where(first, m[:BLK], m[BLK:]))
        accs.append(acc)
    return _zero_after(jnp.concatenate(accs, axis=0))


def _attn_kernel(q_ref, k_ref, v_ref, out_ref, mask_ref, qp_ref, kp_ref, vp_ref, *stats):
    seq = q_ref.shape[0]
    _fill_band_mask(mask_ref)
    for src_ref, plane_ref in ((q_ref, qp_ref), (k_ref, kp_ref), (v_ref, vp_ref)):
        for p in range(ATTN_PLANES):
            for c in range(0, seq // ATTN_PLANES, BLK):
                plane_ref[p, c:c + BLK, :] = src_ref[pl.ds(ATTN_PLANES * c + p, BLK, stride=ATTN_PLANES), :]

    def chains_of(d, plane):
        dilation = DILATIONS[d]
        dst_refs = stats[3 * d:3 * d + 3]
        span = BLK * dilation
        n_blocks = seq // span
        chain = min(n_blocks, ATTN_GROUP)
        chains = []
        for r in range(dilation):
            if plane is not None and r % ATTN_PLANES != plane:
                continue
            for c in range(n_blocks // chain):
                start = c * chain * span + r
                if plane is None:
                    src = (q_ref, k_ref, v_ref, start, dilation, span)
                    dst = dst_refs + (start, dilation, span)
                else:
                    where = (start // ATTN_PLANES, dilation // ATTN_PLANES, span // ATTN_PLANES)
                    src = (qp_ref.at[plane], kp_ref.at[plane], vp_ref.at[plane]) + where
                    dst = tuple(ref.at[plane] for ref in dst_refs) + where
                chains.append(functools.partial(_attn_chain, src, dst, mask_ref, chain, c > 0))
        per_piece = ATTN_GROUP // chain

        def run(order, group):
            return functools.reduce(jnp.add, [one(order) for one in group])
        return [functools.partial(run, group=chains[g:g + per_piece]) for g in range(0, len(chains), per_piece)]

    def mix_piece(order, plane):
        per_plane = seq // ATTN_PLANES

        def rows_of(ref, dilation):
            if dilation % ATTN_PLANES:
                return ref[pl.ds(plane, per_plane, stride=ATTN_PLANES), :]
            return ref[plane]
        accs, ms, sums = ([rows_of(stats[3 * d + i], dil) for d, dil in enumerate(DILATIONS)] for i in range(3))
        if order is not None:
            ms[0] = ms[0] + order
        top = functools.reduce(jnp.maximum, ms)
        es = [jnp.exp(m - top) for m in ms]
        num = functools.reduce(jnp.add, [e * a for e, a in zip(es, accs)])
        den = functools.reduce(jnp.add, [e * l for e, l in zip(es, sums)])
        out = num / den
        out_ref[pl.ds(plane, per_plane, stride=ATTN_PLANES), :] = out
        return _zero_after(out)

    pieces = [piece for d, dil in enumerate(DILATIONS) if dil % ATTN_PLANES for piece in chains_of(d, None)]
    for plane in range(ATTN_PLANES):
        for d, dil in enumerate(DILATIONS):
            if dil % ATTN_PLANES == 0:
                pieces += chains_of(d, plane)
        pieces.append(functools.partial(mix_piece, plane=plane))
    orders = []
    for piece in pieces:
        orders.append(piece(orders[-ATTN_IN_FLIGHT] if len(orders) >= ATTN_IN_FLIGHT else None))


def _dilated_attn(q, k, v):
    B, S, _ = q.shape
    spec = pl.BlockSpec((None, S, PAIR), lambda b, h: (b, 0, h))
    planes = pltpu.VMEM((ATTN_PLANES, S // ATTN_PLANES, PAIR), F32)
    return pl.pallas_call(
        _attn_kernel,
        grid=(B, N_PAIRS),
        in_specs=[spec, spec, spec],
        out_specs=spec,
        out_shape=jax.ShapeDtypeStruct((B, S, D_ATTN), F32),
        scratch_shapes=[pltpu.VMEM((2 * BLK, 2 * BLK), F32)] + [planes] * 3
                       + [pltpu.VMEM((S, PAIR), F32) if d % ATTN_PLANES else planes
                          for d in DILATIONS for _ in range(3)],
        compiler_params=pltpu.CompilerParams(dimension_semantics=("arbitrary", "arbitrary"),
                                             vmem_limit_bytes=VMEM_LIMIT_BYTES),
        name="dilated_attn",
    )(q, k, v)


def _outproj_ffn_kernel(h1_ref, attn_ref, conv_ref, woa_ref, woc_ref, n2_ref, wg_ref, wu_ref, wd_ref,
                        out_ref):
    mix = (jnp.dot(attn_ref[...].astype(BF16), woa_ref[...], preferred_element_type=F32)
           + jnp.dot(conv_ref[...], woc_ref[...], preferred_element_type=F32))
    h2 = h1_ref[...] + mix
    out_ref[...] = h2 + 0.5 * _swiglu(_rms(h2, n2_ref[...]).astype(BF16), wg_ref, wu_ref, wd_ref)


def _outproj_ffn(h1, attn, conv, woa, woc, n2, wg, wu, wd):
    T = h1.shape[0]
    tm = TM_OUT
    row = lambda w: pl.BlockSpec((tm, w), lambda i: (i, 0))
    return pl.pallas_call(
        _outproj_ffn_kernel,
        grid=(T // tm,),
        in_specs=[row(D_MODEL), row(D_ATTN), row(D_CONV),
                  _const_spec((D_ATTN, D_MODEL)), _const_spec((D_CONV, D_MODEL)), _const_spec((1, D_MODEL)),
                  _const_spec((D_MODEL, D_FF)), _const_spec((D_MODEL, D_FF)), _const_spec((D_FF, D_MODEL))],
        out_specs=row(D_MODEL),
        out_shape=jax.ShapeDtypeStruct((T, D_MODEL), F32),
        compiler_params=pltpu.CompilerParams(dimension_semantics=("arbitrary",),
                                             vmem_limit_bytes=VMEM_LIMIT_BYTES),
        name="outproj_ffn",
    )(h1, attn, conv, woa, woc, n2, wg, wu, wd)


def _layer(h, ffn1_norm, ffn1_w_gate, ffn1_w_up, ffn1_w_down, mix_norm, w_in, q_norm, k_norm, conv_w,
           conv_b, conv_ln_g, conv_ln_b, w_out, ffn2_norm, ffn2_w_gate, ffn2_w_up, ffn2_w_down):
    B, S, D = h.shape
    T = B * S
    row = lambda a: a.reshape(1, -1).astype(F32)
    head_id = jnp.arange(D_ATTN) // HEAD_DIM
    head_mean = jnp.where(head_id[:, None] == head_id[None, :], 1.0 / HEAD_DIM, 0.0).astype(BF16)

    h1, q, k, v, conv = _ffn_inproj(
        h.reshape(T, D), S, row(ffn1_norm), ffn1_w_gate.astype(BF16), ffn1_w_up.astype(BF16),
        ffn1_w_down.astype(BF16), row(mix_norm), w_in.astype(BF16),
        row(jnp.tile(q_norm, HEADS)), row(jnp.tile(k_norm, HEADS)), head_mean,
        conv_w.astype(F32), row(conv_b), row(conv_ln_g), row(conv_ln_b))

    attn = _dilated_attn(q.reshape(B, S, D_ATTN), k.reshape(B, S, D_ATTN), v.reshape(B, S, D_ATTN))

    w_out16 = w_out.astype(BF16)
    out = _outproj_ffn(h1, attn.reshape(T, D_ATTN), conv, w_out16[:D_ATTN], w_out16[D_ATTN:],
                       row(ffn2_norm), ffn2_w_gate.astype(BF16), ffn2_w_up.astype(BF16),
                       ffn2_w_down.astype(BF16))
    return out.reshape(B, S, D)


def kernel(x, ffn1_norm, ffn1_w_gate, ffn1_w_up, ffn1_w_down, mix_norm, w_in, q_norm, k_norm, conv_w, conv_b,
           conv_ln_g, conv_ln_b, w_out, ffn2_norm, ffn2_w_gate, ffn2_w_up, ffn2_w_down):
    assert x.shape[1] % (BLK * max(DILATIONS)) == 0 and x.shape[1] % TM_FFN == 0 and x.shape[2] == D_MODEL
    h = x
    for l in range(ffn1_norm.shape[0]):
        h = _layer(h, ffn1_norm[l], ffn1_w_gate[l], ffn1_w_up[l], ffn1_w_down[l], mix_norm[l], w_in[l],
                   q_norm[l], k_norm[l], conv_w[l], conv_b[l], conv_ln_g[l], conv_ln_b[l], w_out[l],
                   ffn2_norm[l], ffn2_w_gate[l], ffn2_w_up[l], ffn2_w_down[l])
    return h
```

```python
import functools

import jax
import jax.numpy as jnp
from jax import lax
from jax.experimental import pallas as pl
from jax.experimental.pallas import tpu as pltpu

F32 = jnp.float32
BF16 = jnp.bfloat16

D_MODEL = 1024
D_FF = 2816
HEADS = 8
HEAD_DIM = 64
D_ATTN = HEADS * HEAD_DIM
D_CONV = 512
D_IN = 3 * D_ATTN + 2 * D_CONV
CONV_K = 31
BLK = 128
DILATIONS = (1, 4, 16)
EPS = 1e-6
NEG = -1e30

LANES = 128
SUBLANES = 8
PAIR = 2 * HEAD_DIM
N_PAIRS = D_ATTN // PAIR
HALO = 32

VMEM_LIMIT_BYTES = 56 * 1024 * 1024

TM_FFN = 512
TM_OUT = 1024
FF_CHUNK = 256
CONV_CHUNK = 64
CONV_SLABS = D_CONV // LANES
ATTN_GROUP = 4
ATTN_PLANES = 4
ATTN_IN_FLIGHT = 6


def _const_spec(shape):
    return pl.BlockSpec(shape, lambda *_: (0,) * len(shape), pipeline_mode=pl.Buffered(1))


def _rms(x, g):
    ms = jnp.mean(x * x, axis=-1, keepdims=True)
    return x * lax.rsqrt(ms + EPS) * g


def _swiglu(xn, wg_ref, wu_ref, wd_ref, side_work=None):
    def after(piece, operand):
        order = side_work(piece) if side_work is not None else None
        if order is None:
            return operand
        head = 2 * SUBLANES
        order = jnp.concatenate([order] * (operand.shape[1] // LANES), axis=1).astype(operand.dtype)
        return jnp.concatenate([operand[:head] + order, operand[head:]], axis=0)

    acc = None
    for c in range(D_FF // FF_CHUNK):
        cols = slice(c * FF_CHUNK, (c + 1) * FF_CHUNK)
        g = jnp.dot(after(3 * c - 1, xn), wg_ref[:, cols], preferred_element_type=F32)
        u = jnp.dot(after(3 * c, xn), wu_ref[:, cols], preferred_element_type=F32)
        a = after(3 * c + 1, g * jax.nn.sigmoid(g) * u)
        part = jnp.dot(a.astype(BF16), wd_ref[cols, :], preferred_element_type=F32)
        acc = part if acc is None else acc + part
    return acc


def _zero_after(x):
    bits = pltpu.bitcast(x, jnp.uint32)
    fold = bits[0:SUBLANES, :]
    for r in range(SUBLANES, x.shape[0], SUBLANES):
        fold = fold | bits[r:r + SUBLANES, :]
    zero = lax.shift_right_logical(lax.shift_right_logical(fold, jnp.uint32(16)), jnp.uint32(16))
    return pltpu.bitcast(zero, F32)[0:1, :]


def _conv_taps(pad_ref, w_ref, row0, lanes, order):
    first = HALO - (CONV_K - 1)
    y = None
    for b in range(SUBLANES):
        rows = CONV_CHUNK + (SUBLANES if b else 0)
        u = None
        for a in range(HALO // SUBLANES + 1):
            j = SUBLANES * a + b - first
            if 0 <= j < CONV_K:
                lo = row0 + SUBLANES * a
                term = pad_ref[lo:lo + rows, lanes] * (w_ref[j:j + 1, lanes] + order)
                u = term if u is None else u + term
        u = u[b:b + CONV_CHUNK, :] if b else u
        y = u if y is None else y + u
    return y


def _conv_piece(piece, order, pad_ref, y_ref, w_ref, b_ref, g_ref, beta_ref, out_ref):
    c, s = divmod(piece, CONV_SLABS)
    rows = slice(c * CONV_CHUNK, (c + 1) * CONV_CHUNK)
    lanes = slice(s * LANES, (s + 1) * LANES)
    y = _conv_taps(pad_ref, w_ref, c * CONV_CHUNK, lanes, order)
    y_ref[rows, lanes] = y + b_ref[:, lanes]
    if s == CONV_SLABS - 1:
        yr = y_ref[rows, :]
        mu = jnp.mean(yr, axis=-1, keepdims=True)
        yc = yr - mu
        var = jnp.mean(yc * yc, axis=-1, keepdims=True)
        z = yc * lax.rsqrt(var + EPS) * g_ref[...] + beta_ref[...]
        out_ref[rows, :] = (z * jax.nn.sigmoid(z)).astype(out_ref.dtype)
    return _zero_after(y)


def _ffn_inproj_kernel(tiles_per_seq, x_ref, n1_ref, wg_ref, wu_ref, wd_ref, nm_ref, win_ref, qg_ref,
                       kg_ref, hm_ref, cw_ref, cb_ref, cg_ref, cbeta_ref,
                       h1_ref, q_ref, k_ref, v_ref, conv_ref, pad_ref, y_ref):
    i = pl.program_id(0)
    tm = x_ref.shape[0]

    @pl.when(i == 0)
    def _():
        pad_ref[...] = jnp.zeros(pad_ref.shape, F32)

    n_pieces = (tm // CONV_CHUNK) * CONV_SLABS
    assert n_pieces <= 3 * (D_FF // FF_CHUNK) - 1
    order = [jnp.zeros((1, LANES), F32)]

    def conv_side_work(piece):
        if not 0 <= piece < n_pieces:
            return None
        order[0] = _conv_piece(piece, order[0], pad_ref, y_ref, cw_ref, cb_ref, cg_ref, cbeta_ref, conv_ref)
        return order[0]

    x = x_ref[...]
    h1 = x + 0.5 * _swiglu(_rms(x, n1_ref[...]).astype(BF16), wg_ref, wu_ref, wd_ref, conv_side_work)
    h1_ref[...] = h1

    u = jnp.dot(_rms(h1, nm_ref[...]).astype(BF16), win_ref[...], preferred_element_type=F32)
    q = u[:, 0:D_ATTN]
    k = u[:, D_ATTN:2 * D_ATTN]
    v_ref[...] = u[:, 2 * D_ATTN:3 * D_ATTN]
    ca = u[:, 3 * D_ATTN:3 * D_ATTN + D_CONV]
    cg = u[:, 3 * D_ATTN + D_CONV:]

    hm = hm_ref[...]
    q_ms = jnp.dot((q * q).astype(BF16), hm, preferred_element_type=F32)
    k_ms = jnp.dot((k * k).astype(BF16), hm, preferred_element_type=F32)
    q_ref[...] = q * lax.rsqrt(q_ms + EPS) * (qg_ref[...] * (HEAD_DIM ** -0.5))
    k_ref[...] = k * lax.rsqrt(k_ms + EPS) * kg_ref[...]

    same_seq = (i % tiles_per_seq) != 0
    pad_ref[0:HALO, :] = jnp.where(same_seq, pad_ref[tm:tm + HALO, :], 0.0)
    pad_ref[HALO:, :] = ca * jax.nn.sigmoid(cg)


def _ffn_inproj(x2, seq, n1, wg, wu, wd, nm, win, qg, kg, hm, cw, cb, cg, cbeta):
    T = x2.shape[0]
    tm = TM_FFN
    n_tiles = T // tm
    cur = lambda w: pl.BlockSpec((tm, w), lambda i: (jnp.minimum(i, n_tiles - 1), 0))
    lag = lambda w: pl.BlockSpec((tm, w), lambda i: (jnp.maximum(i - 1, 0), 0))
    return pl.pallas_call(
        functools.partial(_ffn_inproj_kernel, seq // tm),
        grid=(n_tiles + 1,),
        in_specs=[cur(D_MODEL), _const_spec((1, D_MODEL)),
                  _const_spec((D_MODEL, D_FF)), _const_spec((D_MODEL, D_FF)), _const_spec((D_FF, D_MODEL)),
                  _const_spec((1, D_MODEL)), _const_spec((D_MODEL, D_IN)),
                  _const_spec((1, D_ATTN)), _const_spec((1, D_ATTN)), _const_spec((D_ATTN, D_ATTN)),
                  _const_spec((CONV_K, D_CONV)), _const_spec((1, D_CONV)), _const_spec((1, D_CONV)),
                  _const_spec((1, D_CONV))],
        out_specs=[cur(D_MODEL), cur(D_ATTN), cur(D_ATTN), cur(D_ATTN), lag(D_CONV)],
        out_shape=[jax.ShapeDtypeStruct((T, D_MODEL), F32)] + [jax.ShapeDtypeStruct((T, D_ATTN), F32)] * 3
                  + [jax.ShapeDtypeStruct((T, D_CONV), BF16)],
        scratch_shapes=[pltpu.VMEM((tm + HALO, D_CONV), F32), pltpu.VMEM((tm, D_CONV), F32)],
        compiler_params=pltpu.CompilerParams(dimension_semantics=("arbitrary",),
                                             vmem_limit_bytes=VMEM_LIMIT_BYTES),
        name="ffn_inproj_conv",
    )(x2, n1, wg, wu, wd, nm, win, qg, kg, hm, cw, cb, cg, cbeta)


def _rows(ref, start, stride):
    if stride == 1:
        return ref[pl.ds(start, BLK), :]
    return ref[pl.ds(start, BLK, stride=stride), :]


def _store_rows(ref, start, stride, val):
    if stride == 1:
        ref[pl.ds(start, BLK), :] = val
    else:
        ref[pl.ds(start, BLK, stride=stride), :] = val


def _fill_band_mask(mask_ref):
    qi = lax.broadcasted_iota(jnp.int32, (2 * BLK, 2 * BLK), 0) & (BLK - 1)
    ci = lax.broadcasted_iota(jnp.int32, (2 * BLK, 2 * BLK), 1)
    dist = BLK + qi - ci
    mask_ref[...] = jnp.where((dist >= 0) & (dist <= BLK), 0.0, NEG)


def _attn_chain(src, dst, mask_ref, chain, has_prev, order):
    q_ref, k_ref, v_ref, src0, src_stride, src_step = src
    a_ref, m_ref, l_ref, dst0, dst_stride, dst_step = dst
    lane = lax.broadcasted_iota(jnp.int32, (BLK, PAIR), 1)
    first = lane < HEAD_DIM
    lo = -1 if has_prev else 0
    k_f = [_rows(k_ref, src0 + j * src_step, src_stride) for j in range(lo, chain)]
    v_f = [_rows(v_ref, src0 + j * src_step, src_stride) for j in range(lo, chain)]
    k_t = [k.astype(BF16) for k in k_f]
    v_t = [v.astype(BF16) for v in v_f]
    nt = (((1,), (1,)), ((), ()))
    accs = []
    for j in range(chain):
        q = _rows(q_ref, src0 + j * src_step, src_stride)
        if order is not None:
            q = q + order
        row = dst0 + j * dst_step
        t = j - lo
        if t > 0:
            q2 = jnp.concatenate([jnp.where(first, q, 0.0), jnp.where(first, 0.0, q)], axis=0).astype(BF16)
            k_all = jnp.concatenate([k_t[t - 1], k_t[t]], axis=0)
            v_all = jnp.concatenate([v_t[t - 1], v_t[t]], axis=0)
            v_aug = jnp.concatenate([v_all, jnp.ones(v_all.shape, BF16)], axis=1)
            s = lax.dot_general(q2, k_all, nt, preferred_element_type=F32) + mask_ref[...]
            m = jnp.max(s, axis=-1, keepdims=True)
            p = jnp.exp(s - m).astype(BF16)
            pv = jnp.dot(p, v_aug, preferred_element_type=F32)
            acc = jnp.where(first, pv[:BLK, :PAIR], pv[BLK:, :PAIR])
            den = jnp.where(first, pv[:BLK, PAIR:], pv[BLK:, PAIR:])
            top = jnp.where(first, m[:BLK], m[BLK:])
        else:
            one = jnp.where(first, 1.0, 0.0)
            k_all = jnp.concatenate([jnp.where(first, k_f[0], 0.0), jnp.where(first, 0.0, k_f[0])], axis=0)
            v_all = jnp.concatenate([jnp.where(first, v_f[0], 0.0), jnp.where(first, 0.0, v_f[0])], axis=0)
            v_aug = jnp.concatenate([v_all, jnp.concatenate([one, 1.0 - one], axis=0)], axis=1).astype(BF16)
            own = mask_ref[:BLK, BLK:]
            s = lax.dot_general(q.astype(BF16), k_all.astype(BF16), nt, preferred_element_type=F32)
            s_a, s_b = s[:, :BLK] + own, s[:, BLK:] + own
            m_a = jnp.max(s_a, axis=-1, keepdims=True)
            m_b = jnp.max(s_b, axis=-1, keepdims=True)
            p = jnp.concatenate([jnp.exp(s_a - m_a), jnp.exp(s_b - m_b)], axis=1).astype(BF16)
            pv = jnp.dot(p, v_aug, preferred_element_type=F32)
            acc, den, top = pv[:, :PAIR], pv[:, PAIR:], jnp.where(first, m_a, m_b)
        _store_rows(a_ref, row, dst_stride, acc)
        _store_rows(l_ref, row, dst_stride, den)
        _store_rows(m_ref, row, dst_stride, top)
        accs.append(acc)
    return _zero_after(jnp.concatenate(accs, axis=0))


def _attn_kernel(q_ref, k_ref, v_ref, out_ref, mask_ref, qp_ref, kp_ref, vp_ref, *stats):
    seq = q_ref.shape[0]
    _fill_band_mask(mask_ref)
    for src_ref, plane_ref in ((q_ref, qp_ref), (k_ref, kp_ref), (v_ref, vp_ref)):
        for p in range(ATTN_PLANES):
            for c in range(0, seq // ATTN_PLANES, BLK):
                plane_ref[p, c:c + BLK, :] = src_ref[pl.ds(ATTN_PLANES * c + p, BLK, stride=ATTN_PLANES), :]

    def chains_of(d, plane):
        dilation = DILATIONS[d]
        dst_refs = stats[3 * d:3 * d + 3]
        span = BLK * dilation
        n_blocks = seq // span
        chain = min(n_blocks, ATTN_GROUP)
        chains = []
        for r in range(dilation):
            if plane is not None and r % ATTN_PLANES != plane:
                continue
            for c in range(n_blocks // chain):
                start = c * chain * span + r
                if plane is None:
                    src = (q_ref, k_ref, v_ref, start, dilation, span)
                    dst = dst_refs + (start, dilation, span)
                else:
                    where = (start // ATTN_PLANES, dilation // ATTN_PLANES, span // ATTN_PLANES)
                    src = (qp_ref.at[plane], kp_ref.at[plane], vp_ref.at[plane]) + where
                    dst = tuple(ref.at[plane] for ref in dst_refs) + where
                chains.append(functools.partial(_attn_chain, src, dst, mask_ref, chain, c > 0))
        per_piece = ATTN_GROUP // chain

        def run(order, group):
            return functools.reduce(jnp.add, [one(order) for one in group])
        return [functools.partial(run, group=chains[g:g + per_piece]) for g in range(0, len(chains), per_piece)]

    def mix_piece(order, plane):
        per_plane = seq // ATTN_PLANES

        def rows_of(ref, dilation):
            if dilation % ATTN_PLANES:
                return ref[pl.ds(plane, per_plane, stride=ATTN_PLANES), :]
            return ref[plane]
        accs, ms, sums = ([rows_of(stats[3 * d + i], dil) for d, dil in enumerate(DILATIONS)] for i in range(3))
        if order is not None:
            ms[0] = ms[0] + order
        top = functools.reduce(jnp.maximum, ms)
        es = [jnp.exp(m - top) for m in ms]
        num = functools.reduce(jnp.add, [e * a for e, a in zip(es, accs)])
        den = functools.reduce(jnp.add, [e * l for e, l in zip(es, sums)])
        out = num / den
        out_ref[pl.ds(plane, per_plane, stride=ATTN_PLANES), :] = out
        return _zero_after(out)

    pieces = [piece for d, dil in enumerate(DILATIONS) if dil % ATTN_PLANES for piece in chains_of(d, None)]
    for plane in range(ATTN_PLANES):
        for d, dil in enumerate(DILATIONS):
            if dil % ATTN_PLANES == 0:
                pieces += chains_of(d, plane)
        pieces.append(functools.partial(mix_piece, plane=plane))
    orders = []
    for piece in pieces:
        orders.append(piece(orders[-ATTN_IN_FLIGHT] if len(orders) >= ATTN_IN_FLIGHT else None))


def _dilated_attn(q, k, v):
    B, S, _ = q.shape
    spec = pl.BlockSpec((None, S, PAIR), lambda b, h: (b, 0, h))
    planes = pltpu.VMEM((ATTN_PLANES, S // ATTN_PLANES, PAIR), F32)
    return pl.pallas_call(
        _attn_kernel,
        grid=(B, N_PAIRS),
        in_specs=[spec, spec, spec],
        out_specs=spec,
        out_shape=jax.ShapeDtypeStruct((B, S, D_ATTN), F32),
        scratch_shapes=[pltpu.VMEM((2 * BLK, 2 * BLK), F32)] + [planes] * 3
                       + [pltpu.VMEM((S, PAIR), F32) if d % ATTN_PLANES else planes
                          for d in DILATIONS for _ in range(3)],
        compiler_params=pltpu.CompilerParams(dimension_semantics=("arbitrary", "arbitrary"),
                                             vmem_limit_bytes=VMEM_LIMIT_BYTES),
        name="dilated_attn",
    )(q, k, v)


def _outproj_ffn_kernel(h1_ref, attn_ref, conv_ref, woa_ref, woc_ref, n2_ref, wg_ref, wu_ref, wd_ref,
                        out_ref):
    mix = (jnp.dot(attn_ref[...].astype(BF16), woa_ref[...], preferred_element_type=F32)
           + jnp.dot(conv_ref[...], woc_ref[...], preferred_element_type=F32))
    h2 = h1_ref[...] + mix
    out_ref[...] = h2 + 0.5 * _swiglu(_rms(h2, n2_ref[...]).astype(BF16), wg_ref, wu_ref, wd_ref)


def _outproj_ffn(h1, attn, conv, woa, woc, n2, wg, wu, wd):
    T = h1.shape[0]
    tm = TM_OUT
    row = lambda w: pl.BlockSpec((tm, w), lambda i: (i, 0))
    return pl.pallas_call(
        _outproj_ffn_kernel,
        grid=(T // tm,),
        in_specs=[row(D_MODEL), row(D_ATTN), row(D_CONV),
                  _const_spec((D_ATTN, D_MODEL)), _const_spec((D_CONV, D_MODEL)), _const_spec((1, D_MODEL)),
                  _const_spec((D_MODEL, D_FF)), _const_spec((D_MODEL, D_FF)), _const_spec((D_FF, D_MODEL))],
        out_specs=row(D_MODEL),
        out_shape=jax.ShapeDtypeStruct((T, D_MODEL), F32),
        compiler_params=pltpu.CompilerParams(dimension_semantics=("arbitrary",),
                                             vmem_limit_bytes=VMEM_LIMIT_BYTES),
        name="outproj_ffn",
    )(h1, attn, conv, woa, woc, n2, wg, wu, wd)


def _layer(h, ffn1_norm, ffn1_w_gate, ffn1_w_up, ffn1_w_down, mix_norm, w_in, q_norm, k_norm, conv_w,
           conv_b, conv_ln_g, conv_ln_b, w_out, ffn2_norm, ffn2_w_gate, ffn2_w_up, ffn2_w_down):
    B, S, D = h.shape
    T = B * S
    row = lambda a: a.reshape(1, -1).astype(F32)
    head_id = jnp.arange(D_ATTN) // HEAD_DIM
    head_mean = jnp.where(head_id[:, None] == head_id[None, :], 1.0 / HEAD_DIM, 0.0).astype(BF16)

    h1, q, k, v, conv = _ffn_inproj(
        h.reshape(T, D), S, row(ffn1_norm), ffn1_w_gate.astype(BF16), ffn1_w_up.astype(BF16),
        ffn1_w_down.astype(BF16), row(mix_norm), w_in.astype(BF16),
        row(jnp.tile(q_norm, HEADS)), row(jnp.tile(k_norm, HEADS)), head_mean,
        conv_w.astype(F32), row(conv_b), row(conv_ln_g), row(conv_ln_b))

    attn = _dilated_attn(q.reshape(B, S, D_ATTN), k.reshape(B, S, D_ATTN), v.reshape(B, S, D_ATTN))

    w_out16 = w_out.astype(BF16)
    out = _outproj_ffn(h1, attn.reshape(T, D_ATTN), conv, w_out16[:D_ATTN], w_out16[D_ATTN:],
                       row(ffn2_norm), ffn2_w_gate.astype(BF16), ffn2_w_up.astype(BF16),
                       ffn2_w_down.astype(BF16))
    return out.reshape(B, S, D)


def kernel(x, ffn1_norm, ffn1_w_gate, ffn1_w_up, ffn1_w_down, mix_norm, w_in, q_norm, k_norm, conv_w, conv_b,
           conv_ln_g, conv_ln_b, w_out, ffn2_norm, ffn2_w_gate, ffn2_w_up, ffn2_w_down):
    assert x.shape[1] % (BLK * max(DILATIONS)) == 0 and x.shape[1] % TM_FFN == 0 and x.shape[2] == D_MODEL
    h = x
    for l in range(ffn1_norm.shape[0]):
        h = _layer(h, ffn1_norm[l], ffn1_w_gate[l], ffn1_w_up[l], ffn1_w_down[l], mix_norm[l], w_in[l],
                   q_norm[l], k_norm[l], conv_w[l], conv_b[l], conv_ln_g[l], conv_ln_b[l], w_out[l],
                   ffn2_norm[l], ffn2_w_gate[l], ffn2_w_up[l], ffn2_w_down[l])
    return h
```

```python
import functools

import jax
import jax.numpy as jnp
from jax import lax
from jax.experimental import pallas as pl
from jax.experimental.pallas import tpu as pltpu

F32 = jnp.float32
BF16 = jnp.bfloat16

D_MODEL = 1024
D_FF = 2816
HEADS = 8
HEAD_DIM = 64
D_ATTN = HEADS * HEAD_DIM
D_CONV = 512
D_IN = 3 * D_ATTN + 2 * D_CONV
CONV_K = 31
BLK = 128
DILATIONS = (1, 4, 16)
EPS = 1e-6
NEG = -1e30
LOG2_E = 1.4426950408889634

LANES = 128
SUBLANES = 8
PAIR = 2 * HEAD_DIM
N_PAIRS = D_ATTN // PAIR
HALO = 32

VMEM_LIMIT_BYTES = 56 * 1024 * 1024

TM_FFN = 512
TM_OUT = 1024
FF_CHUNK = 256
CONV_CHUNK = 64
CONV_SLABS = D_CONV // LANES
ATTN_GROUP = 4
ATTN_PLANES = 4
ATTN_IN_FLIGHT = 6


def _const_spec(shape):
    return pl.BlockSpec(shape, lambda *_: (0,) * len(shape), pipeline_mode=pl.Buffered(1))


def _rms(x, g):
    ms = jnp.mean(x * x, axis=-1, keepdims=True)
    return x * lax.rsqrt(ms + EPS) * g


def _swiglu(xn, wg_ref, wu_ref, wd_ref, side_work=None):
    def after(piece, operand):
        order = side_work(piece) if side_work is not None else None
        if order is None:
            return operand
        head = 2 * SUBLANES
        order = jnp.concatenate([order] * (operand.shape[1] // LANES), axis=1).astype(operand.dtype)
        return jnp.concatenate([operand[:head] + order, operand[head:]], axis=0)

    acc = None
    for c in range(D_FF // FF_CHUNK):
        cols = slice(c * FF_CHUNK, (c + 1) * FF_CHUNK)
        g = jnp.dot(after(3 * c - 1, xn), wg_ref[:, cols], preferred_element_type=F32)
        u = jnp.dot(after(3 * c, xn), wu_ref[:, cols], preferred_element_type=F32)
        a = after(3 * c + 1, g * jax.nn.sigmoid(g) * u)
        part = jnp.dot(a.astype(BF16), wd_ref[cols, :], preferred_element_type=F32)
        acc = part if acc is None else acc + part
    return acc


def _zero_after(x):
    bits = pltpu.bitcast(x, jnp.uint32)
    fold = bits[0:SUBLANES, :]
    for r in range(SUBLANES, x.shape[0], SUBLANES):
        fold = fold | bits[r:r + SUBLANES, :]
    zero = lax.shift_right_logical(lax.shift_right_logical(fold, jnp.uint32(16)), jnp.uint32(16))
    return pltpu.bitcast(zero, F32)[0:1, :]


def _conv_taps(pad_ref, w_ref, row0, lanes, order):
    first = HALO - (CONV_K - 1)
    y = None
    for b in range(SUBLANES):
        rows = CONV_CHUNK + (SUBLANES if b else 0)
        u = None
        for a in range(HALO // SUBLANES + 1):
            j = SUBLANES * a + b - first
            if 0 <= j < CONV_K:
                lo = row0 + SUBLANES * a
                term = pad_ref[lo:lo + rows, lanes] * (w_ref[j:j + 1, lanes] + order)
                u = term if u is None else u + term
        u = u[b:b + CONV_CHUNK, :] if b else u
        y = u if y is None else y + u
    return y


def _conv_piece(piece, order, pad_ref, y_ref, w_ref, b_ref, g_ref, beta_ref, out_ref):
    c, s = divmod(piece, CONV_SLABS)
    rows = slice(c * CONV_CHUNK, (c + 1) * CONV_CHUNK)
    lanes = slice(s * LANES, (s + 1) * LANES)
    y = _conv_taps(pad_ref, w_ref, c * CONV_CHUNK, lanes, order)
    y_ref[rows, lanes] = y + b_ref[:, lanes]
    if s == CONV_SLABS - 1:
        yr = y_ref[rows, :]
        mu = jnp.mean(yr, axis=-1, keepdims=True)
        yc = yr - mu
        var = jnp.mean(yc * yc, axis=-1, keepdims=True)
        z = yc * lax.rsqrt(var + EPS) * g_ref[...] + beta_ref[...]
        out_ref[rows, :] = (z * jax.nn.sigmoid(z)).astype(out_ref.dtype)
    return _zero_after(y)


def _ffn_inproj_kernel(tiles_per_seq, x_ref, n1_ref, wg_ref, wu_ref, wd_ref, nm_ref, win_ref, qg_ref,
                       kg_ref, hm_ref, cw_ref, cb_ref, cg_ref, cbeta_ref,
                       h1_ref, q_ref, k_ref, v_ref, conv_ref, pad_ref, y_ref):
    i = pl.program_id(0)
    tm = x_ref.shape[0]

    @pl.when(i == 0)
    def _():
        pad_ref[...] = jnp.zeros(pad_ref.shape, F32)

    n_pieces = (tm // CONV_CHUNK) * CONV_SLABS
    assert n_pieces <= 3 * (D_FF // FF_CHUNK) - 1
    order = [jnp.zeros((1, LANES), F32)]

    def conv_side_work(piece):
        if not 0 <= piece < n_pieces:
            return None
        order[0] = _conv_piece(piece, order[0], pad_ref, y_ref, cw_ref, cb_ref, cg_ref, cbeta_ref, conv_ref)
        return order[0]

    x = x_ref[...]
    h1 = x + 0.5 * _swiglu(_rms(x, n1_ref[...]).astype(BF16), wg_ref, wu_ref, wd_ref, conv_side_work)
    h1_ref[...] = h1

    u = jnp.dot(_rms(h1, nm_ref[...]).astype(BF16), win_ref[...], preferred_element_type=F32)
    q = u[:, 0:D_ATTN]
    k = u[:, D_ATTN:2 * D_ATTN]
    v_ref[...] = u[:, 2 * D_ATTN:3 * D_ATTN]
    ca = u[:, 3 * D_ATTN:3 * D_ATTN + D_CONV]
    cg = u[:, 3 * D_ATTN + D_CONV:]

    hm = hm_ref[...]
    q_ms = jnp.dot((q * q).astype(BF16), hm, preferred_element_type=F32)
    k_ms = jnp.dot((k * k).astype(BF16), hm, preferred_element_type=F32)
    q_ref[...] = q * lax.rsqrt(q_ms + EPS) * (qg_ref[...] * (HEAD_DIM ** -0.5 * LOG2_E))
    k_ref[...] = k * lax.rsqrt(k_ms + EPS) * kg_ref[...]

    same_seq = (i % tiles_per_seq) != 0
    pad_ref[0:HALO, :] = jnp.where(same_seq, pad_ref[tm:tm + HALO, :], 0.0)
    pad_ref[HALO:, :] = ca * jax.nn.sigmoid(cg)


def _ffn_inproj(x2, seq, n1, wg, wu, wd, nm, win, qg, kg, hm, cw, cb, cg, cbeta):
    T = x2.shape[0]
    tm = TM_FFN
    n_tiles = T // tm
    cur = lambda w: pl.BlockSpec((tm, w), lambda i: (jnp.minimum(i, n_tiles - 1), 0))
    lag = lambda w: pl.BlockSpec((tm, w), lambda i: (jnp.maximum(i - 1, 0), 0))
    return pl.pallas_call(
        functools.partial(_ffn_inproj_kernel, seq // tm),
        grid=(n_tiles + 1,),
        in_specs=[cur(D_MODEL), _const_spec((1, D_MODEL)),
                  _const_spec((D_MODEL, D_FF)), _const_spec((D_MODEL, D_FF)), _const_spec((D_FF, D_MODEL)),
                  _const_spec((1, D_MODEL)), _const_spec((D_MODEL, D_IN)),
                  _const_spec((1, D_ATTN)), _const_spec((1, D_ATTN)), _const_spec((D_ATTN, D_ATTN)),
                  _const_spec((CONV_K, D_CONV)), _const_spec((1, D_CONV)), _const_spec((1, D_CONV)),
                  _const_spec((1, D_CONV))],
        out_specs=[cur(D_MODEL), cur(D_ATTN), cur(D_ATTN), cur(D_ATTN), lag(D_CONV)],
        out_shape=[jax.ShapeDtypeStruct((T, D_MODEL), F32)] + [jax.ShapeDtypeStruct((T, D_ATTN), F32)] * 3
                  + [jax.ShapeDtypeStruct((T, D_CONV), BF16)],
        scratch_shapes=[pltpu.VMEM((tm + HALO, D_CONV), F32), pltpu.VMEM((tm, D_CONV), F32)],
        compiler_params=pltpu.CompilerParams(dimension_semantics=("arbitrary",),
                                             vmem_limit_bytes=VMEM_LIMIT_BYTES),
        name="ffn_inproj_conv",
    )(x2, n1, wg, wu, wd, nm, win, qg, kg, hm, cw, cb, cg, cbeta)


def _rows(ref, start, stride):
    if stride == 1:
        return ref[pl.ds(start, BLK), :]
    return ref[pl.ds(start, BLK, stride=stride), :]


def _store_rows(ref, start, stride, val):
    if stride == 1:
        ref[pl.ds(start, BLK), :] = val
    else:
        ref[pl.ds(start, BLK, stride=stride), :] = val


def _fill_band_mask(mask_ref):
    qi = lax.broadcasted_iota(jnp.int32, (2 * BLK, 2 * BLK), 0) & (BLK - 1)
    ci = lax.broadcasted_iota(jnp.int32, (2 * BLK, 2 * BLK), 1)
    dist = BLK + qi - ci
    mask_ref[...] = jnp.where((dist >= 0) & (dist <= BLK), 0.0, NEG)


def _attn_chain(src, dst, mask_ref, chain, has_prev, order):
    q_ref, k_ref, v_ref, src0, src_stride, src_step = src
    a_ref, m_ref, l_ref, dst0, dst_stride, dst_step = dst
    lane = lax.broadcasted_iota(jnp.int32, (BLK, PAIR), 1)
    first = lane < HEAD_DIM
    lo = -1 if has_prev else 0
    k_f = [_rows(k_ref, src0 + j * src_step, src_stride) for j in range(lo, chain)]
    v_f = [_rows(v_ref, src0 + j * src_step, src_stride) for j in range(lo, chain)]
    k_t = [k.astype(BF16) for k in k_f]
    v_t = [v.astype(BF16) for v in v_f]
    nt = (((1,), (1,)), ((), ()))
    accs = []
    for j in range(chain):
        q = _rows(q_ref, src0 + j * src_step, src_stride)
        if order is not None:
            q = jnp.concatenate([q[:SUBLANES] + order, q[SUBLANES:]], axis=0)
        row = dst0 + j * dst_step
        t = j - lo
        if t > 0:
            q2 = jnp.concatenate([jnp.where(first, q, 0.0), jnp.where(first, 0.0, q)], axis=0).astype(BF16)
            k_all = jnp.concatenate([k_t[t - 1], k_t[t]], axis=0)
            v_all = jnp.concatenate([v_t[t - 1], v_t[t]], axis=0)
            v_aug = jnp.concatenate([v_all, jnp.ones(v_all.shape, BF16)], axis=1)
            s = lax.dot_general(q2, k_all, nt, preferred_element_type=F32) + mask_ref[...]
            m = jnp.max(s, axis=-1, keepdims=True)
            p = jnp.exp2(s - m).astype(BF16)
            pv = jnp.dot(p, v_aug, preferred_element_type=F32)
            acc = jnp.where(first, pv[:BLK, :PAIR], pv[BLK:, :PAIR])
            den = jnp.where(first, pv[:BLK, PAIR:], pv[BLK:, PAIR:])
            top = jnp.where(first, m[:BLK], m[BLK:])
        else:
            one = jnp.where(first, 1.0, 0.0)
            k_all = jnp.concatenate([jnp.where(first, k_f[0], 0.0), jnp.where(first, 0.0, k_f[0])], axis=0)
            v_all = jnp.concatenate([jnp.where(first, v_f[0], 0.0), jnp.where(first, 0.0, v_f[0])], axis=0)
            v_aug = jnp.concatenate([v_all, jnp.concatenate([one, 1.0 - one], axis=0)], axis=1).astype(BF16)
            own = mask_ref[:BLK, BLK:]
            s = lax.dot_general(q.astype(BF16), k_all.astype(BF16), nt, preferred_element_type=F32)
            s_a, s_b = s[:, :BLK] + own, s[:, BLK:] + own
            m_a = jnp.max(s_a, axis=-1, keepdims=True)
            m_b = jnp.max(s_b, axis=-1, keepdims=True)
            p = jnp.concatenate([jnp.exp2(s_a - m_a), jnp.exp2(s_b - m_b)], axis=1).astype(BF16)
            pv = jnp.dot(p, v_aug, preferred_element_type=F32)
            acc, den, top = pv[:, :PAIR], pv[:, PAIR:], jnp.where(first, m_a, m_b)
        _store_rows(a_ref, row, dst_stride, acc)
        _store_rows(l_ref, row, dst_stride, den)
        _store_rows(m_ref, row, dst_stride, top)
        accs.append(acc)
    return _zero_after(jnp.concatenate(accs, axis=0))


def _attn_kernel(q_ref, k_ref, v_ref, out_ref, mask_ref, qp_ref, kp_ref, vp_ref, *stats):
    seq = q_ref.shape[0]
    _fill_band_mask(mask_ref)
    for src_ref, plane_ref in ((q_ref, qp_ref), (k_ref, kp_ref), (v_ref, vp_ref)):
        for p in range(ATTN_PLANES):
            for c in range(0, seq // ATTN_PLANES, BLK):
                plane_ref[p, c:c + BLK, :] = src_ref[pl.ds(ATTN_PLANES * c + p, BLK, stride=ATTN_PLANES), :]

    def chains_of(d, plane):
        dilation = DILATIONS[d]
        dst_refs = stats[3 * d:3 * d + 3]
        span = BLK * dilation
        n_blocks = seq // span
        chain = min(n_blocks, ATTN_GROUP)
        chains = []
        for r in range(dilation):
            if plane is not None and r % ATTN_PLANES != plane:
                continue
            for c in range(n_blocks // chain):
                start = c * chain * span + r
                if plane is None:
                    src = (q_ref, k_ref, v_ref, start, dilation, span)
                    dst = dst_refs + (start, dilation, span)
                else:
                    where = (start // ATTN_PLANES, dilation // ATTN_PLANES, span // ATTN_PLANES)
                    src = (qp_ref.at[plane], kp_ref.at[plane], vp_ref.at[plane]) + where
                    dst = tuple(ref.at[plane] for ref in dst_refs) + where
                chains.append(functools.partial(_attn_chain, src, dst, mask_ref, chain, c > 0))
        per_piece = ATTN_GROUP // chain

        def run(order, group):
            return functools.reduce(jnp.add, [one(order) for one in group])
        return [functools.partial(run, group=chains[g:g + per_piece]) for g in range(0, len(chains), per_piece)]

    def mix_piece(order, plane):
        per_plane = seq // ATTN_PLANES

        def rows_of(ref, dilation):
            if dilation % ATTN_PLANES:
                return ref[pl.ds(plane, per_plane, stride=ATTN_PLANES), :]
            return ref[plane]
        accs, ms, sums = ([rows_of(stats[3 * d + i], dil) for d, dil in enumerate(DILATIONS)] for i in range(3))
        if order is not None:
            ms[0] = ms[0] + order
        top = functools.reduce(jnp.maximum, ms)
        es = [jnp.exp2(m - top) for m in ms]
        num = functools.reduce(jnp.add, [e * a for e, a in zip(es, accs)])
        den = functools.reduce(jnp.add, [e * l for e, l in zip(es, sums)])
        out = num / den
        out_ref[pl.ds(plane, per_plane, stride=ATTN_PLANES), :] = out
        return _zero_after(out)

    pieces = [piece for d, dil in enumerate(DILATIONS) if dil % ATTN_PLANES for piece in chains_of(d, None)]
    for plane in range(ATTN_PLANES):
        for d, dil in enumerate(DILATIONS):
            if dil % ATTN_PLANES == 0:
                pieces += chains_of(d, plane)
        pieces.append(functools.partial(mix_piece, plane=plane))
    orders = []
    for piece in pieces:
        orders.append(piece(orders[-ATTN_IN_FLIGHT] if len(orders) >= ATTN_IN_FLIGHT else None))


def _dilated_attn(q, k, v):
    B, S, _ = q.shape
    spec = pl.BlockSpec((None, S, PAIR), lambda b, h: (b, 0, h))
    planes = pltpu.VMEM((ATTN_PLANES, S // ATTN_PLANES, PAIR), F32)
    return pl.pallas_call(
        _attn_kernel,
        grid=(B, N_PAIRS),
        in_specs=[spec, spec, spec],
        out_specs=spec,
        out_shape=jax.ShapeDtypeStruct((B, S, D_ATTN), F32),
        scratch_shapes=[pltpu.VMEM((2 * BLK, 2 * BLK), F32)] + [planes] * 3
                       + [pltpu.VMEM((S, PAIR), F32) if d % ATTN_PLANES else planes
                          for d in DILATIONS for _ in range(3)],
        compiler_params=pltpu.CompilerParams(dimension_semantics=("arbitrary", "arbitrary"),
                                             vmem_limit_bytes=VMEM_LIMIT_BYTES),
        name="dilated_attn",
    )(q, k, v)


def _outproj_ffn_kernel(h1_ref, attn_ref, conv_ref, woa_ref, woc_ref, n2_ref, wg_ref, wu_ref, wd_ref,
                        out_ref):
    mix = (jnp.dot(attn_ref[...].astype(BF16), woa_ref[...], preferred_element_type=F32)
           + jnp.dot(conv_ref[...], woc_ref[...], preferred_element_type=F32))
    h2 = h1_ref[...] + mix
    out_ref[...] = h2 + 0.5 * _swiglu(_rms(h2, n2_ref[...]).astype(BF16), wg_ref, wu_ref, wd_ref)


def _outproj_ffn(h1, attn, conv, woa, woc, n2, wg, wu, wd):
    T = h1.shape[0]
    tm = TM_OUT
    row = lambda w: pl.BlockSpec((tm, w), lambda i: (i, 0))
    return pl.pallas_call(
        _outproj_ffn_kernel,
        grid=(T // tm,),
        in_specs=[row(D_MODEL), row(D_ATTN), row(D_CONV),
                  _const_spec((D_ATTN, D_MODEL)), _const_spec((D_CONV, D_MODEL)), _const_spec((1, D_MODEL)),
                  _const_spec((D_MODEL, D_FF)), _const_spec((D_MODEL, D_FF)), _const_spec((D_FF, D_MODEL))],
        out_specs=row(D_MODEL),
        out_shape=jax.ShapeDtypeStruct((T, D_MODEL), F32),
        compiler_params=pltpu.CompilerParams(dimension_semantics=("arbitrary",),
                                             vmem_limit_bytes=VMEM_LIMIT_BYTES),
        name="outproj_ffn",
    )(h1, attn, conv, woa, woc, n2, wg, wu, wd)


def _layer(h, ffn1_norm, ffn1_w_gate, ffn1_w_up, ffn1_w_down, mix_norm, w_in, q_norm, k_norm, conv_w,
           conv_b, conv_ln_g, conv_ln_b, w_out, ffn2_norm, ffn2_w_gate, ffn2_w_up, ffn2_w_down):
    B, S, D = h.shape
    T = B * S
    row = lambda a: a.reshape(1, -1).astype(F32)
    head_id = jnp.arange(D_ATTN) // HEAD_DIM
    head_mean = jnp.where(head_id[:, None] == head_id[None, :], 1.0 / HEAD_DIM, 0.0).astype(BF16)

    h1, q, k, v, conv = _ffn_inproj(
        h.reshape(T, D), S, row(ffn1_norm), ffn1_w_gate.astype(BF16), ffn1_w_up.astype(BF16),
        ffn1_w_down.astype(BF16), row(mix_norm), w_in.astype(BF16),
        row(jnp.tile(q_norm, HEADS)), row(jnp.tile(k_norm, HEADS)), head_mean,
        conv_w.astype(F32), row(conv_b), row(conv_ln_g), row(conv_ln_b))

    attn = _dilated_attn(q.reshape(B, S, D_ATTN), k.reshape(B, S, D_ATTN), v.reshape(B, S, D_ATTN))

    w_out16 = w_out.astype(BF16)
    out = _outproj_ffn(h1, attn.reshape(T, D_ATTN), conv, w_out16[:D_ATTN], w_out16[D_ATTN:],
                       row(ffn2_norm), ffn2_w_gate.astype(BF16), ffn2_w_up.astype(BF16),
                       ffn2_w_down.astype(BF16))
    return out.reshape(B, S, D)


def kernel(x, ffn1_norm, ffn1_w_gate, ffn1_w_up, ffn1_w_down, mix_norm, w_in, q_norm, k_norm, conv_w, conv_b,
           conv_ln_g, conv_ln_b, w_out, ffn2_norm, ffn2_w_gate, ffn2_w_up, ffn2_w_down):
    assert x.shape[1] % (BLK * max(DILATIONS)) == 0 and x.shape[1] % TM_FFN == 0 and x.shape[2] == D_MODEL
    h = x
    for l in range(ffn1_norm.shape[0]):
        h = _layer(h, ffn1_norm[l], ffn1_w_gate[l], ffn1_w_up[l], ffn1_w_down[l], mix_norm[l], w_in[l],
                   q_norm[l], k_norm[l], conv_w[l], conv_b[l], conv_ln_g[l], conv_ln_b[l], w_out[l],
                   ffn2_norm[l], ffn2_w_gate[l], ffn2_w_up[l], ffn2_w_down[l])
    return h
```

```python
import functools

import jax
import jax.numpy as jnp
from jax import lax
from jax.experimental import pallas as pl
from jax.experimental.pallas import tpu as pltpu

F32 = jnp.float32
BF16 = jnp.bfloat16

D_MODEL = 1024
D_FF = 2816
HEADS = 8
HEAD_DIM = 64
D_ATTN = HEADS * HEAD_DIM
D_CONV = 512
D_IN = 3 * D_ATTN + 2 * D_CONV
CONV_K = 31
BLK = 128
DILATIONS = (1, 4, 16)
EPS = 1e-6
NEG = -1e30
LOG2_E = 1.4426950408889634

LANES = 128
SUBLANES = 8
PAIR = 2 * HEAD_DIM
N_PAIRS = D_ATTN // PAIR
HALO = 32

VMEM_LIMIT_BYTES = 56 * 1024 * 1024

TM_FFN = 512
TM_OUT = 1024
FF_CHUNK = 256
CONV_CHUNK = 64
CONV_SLABS = D_CONV // LANES
ATTN_GROUP = 4
ATTN_PLANES = 4
ATTN_IN_FLIGHT = 6


def _const_spec(shape):
    return pl.BlockSpec(shape, lambda *_: (0,) * len(shape), pipeline_mode=pl.Buffered(1))


def _rms(x, g):
    ms = jnp.mean(x * x, axis=-1, keepdims=True)
    return x * lax.rsqrt(ms + EPS) * g


def _swiglu(xn, wg_ref, wu_ref, wd_ref, side_work=None):
    def after(piece, operand):
        order = side_work(piece) if side_work is not None else None
        if order is None:
            return operand
        head = 2 * SUBLANES
        order = jnp.concatenate([order] * (operand.shape[1] // LANES), axis=1).astype(operand.dtype)
        return jnp.concatenate([operand[:head] + order, operand[head:]], axis=0)

    acc = None
    for c in range(D_FF // FF_CHUNK):
        cols = slice(c * FF_CHUNK, (c + 1) * FF_CHUNK)
        g = jnp.dot(after(3 * c - 1, xn), wg_ref[:, cols], preferred_element_type=F32)
        u = jnp.dot(after(3 * c, xn), wu_ref[:, cols], preferred_element_type=F32)
        a = after(3 * c + 1, g * jax.nn.sigmoid(g) * u)
        part = jnp.dot(a.astype(BF16), wd_ref[cols, :], preferred_element_type=F32)
        acc = part if acc is None else acc + part
    return acc


def _zero_after(x):
    bits = pltpu.bitcast(x, jnp.uint32)
    fold = bits[0:SUBLANES, :]
    for r in range(SUBLANES, x.shape[0], SUBLANES):
        fold = fold | bits[r:r + SUBLANES, :]
    zero = lax.shift_right_logical(lax.shift_right_logical(fold, jnp.uint32(16)), jnp.uint32(16))
    return pltpu.bitcast(zero, F32)[0:1, :]


def _conv_taps(pad_ref, w_ref, row0, lanes, order):
    first = HALO - (CONV_K - 1)
    y = None
    for b in range(SUBLANES):
        rows = CONV_CHUNK + (SUBLANES if b else 0)
        u = None
        for a in range(HALO // SUBLANES + 1):
            j = SUBLANES * a + b - first
            if 0 <= j < CONV_K:
                lo = row0 + SUBLANES * a
                term = pad_ref[lo:lo + rows, lanes] * (w_ref[j:j + 1, lanes] + order)
                u = term if u is None else u + term
        u = u[b:b + CONV_CHUNK, :] if b else u
        y = u if y is None else y + u
    return y


def _conv_piece(piece, order, pad_ref, y_ref, w_ref, b_ref, g_ref, beta_ref, out_ref):
    c, s = divmod(piece, CONV_SLABS)
    rows = slice(c * CONV_CHUNK, (c + 1) * CONV_CHUNK)
    lanes = slice(s * LANES, (s + 1) * LANES)
    y = _conv_taps(pad_ref, w_ref, c * CONV_CHUNK, lanes, order)
    y_ref[rows, lanes] = y + b_ref[:, lanes]
    if s == CONV_SLABS - 1:
        yr = y_ref[rows, :]
        mu = jnp.mean(yr, axis=-1, keepdims=True)
        yc = yr - mu
        var = jnp.mean(yc * yc, axis=-1, keepdims=True)
        z = yc * lax.rsqrt(var + EPS) * g_ref[...] + beta_ref[...]
        out_ref[rows, :] = (z * jax.nn.sigmoid(z)).astype(out_ref.dtype)
    return _zero_after(y)


def _ffn_inproj_kernel(tiles_per_seq, x_ref, n1_ref, wg_ref, wu_ref, wd_ref, nm_ref, win_ref, qg_ref,
                       kg_ref, hm_ref, cw_ref, cb_ref, cg_ref, cbeta_ref,
                       h1_ref, q_ref, k_ref, v_ref, conv_ref, pad_ref, y_ref):
    i = pl.program_id(0)
    tm = x_ref.shape[0]

    @pl.when(i == 0)
    def _():
        pad_ref[...] = jnp.zeros(pad_ref.shape, F32)

    n_pieces = (tm // CONV_CHUNK) * CONV_SLABS
    assert n_pieces <= 3 * (D_FF // FF_CHUNK) - 1
    order = [jnp.zeros((1, LANES), F32)]

    def conv_side_work(piece):
        if not 0 <= piece < n_pieces:
            return None
        order[0] = _conv_piece(piece, order[0], pad_ref, y_ref, cw_ref, cb_ref, cg_ref, cbeta_ref, conv_ref)
        return order[0]

    x = x_ref[...]
    h1 = x + 0.5 * _swiglu(_rms(x, n1_ref[...]).astype(BF16), wg_ref, wu_ref, wd_ref, conv_side_work)
    h1_ref[...] = h1

    hn = _rms(h1, nm_ref[...]).astype(BF16)

    def proj(col, width):
        return jnp.dot(hn, win_ref[:, col:col + width], preferred_element_type=F32)

    cg = proj(3 * D_ATTN + D_CONV, D_CONV)
    ca = proj(3 * D_ATTN, D_CONV)
    same_seq = (i % tiles_per_seq) != 0
    pad_ref[0:HALO, :] = jnp.where(same_seq, pad_ref[tm:tm + HALO, :], 0.0)
    pad_ref[HALO:, :] = ca * jax.nn.sigmoid(cg)

    v_ref[...] = proj(2 * D_ATTN, D_ATTN)

    hm = hm_ref[...]
    q = proj(0, D_ATTN)
    q_ms = jnp.dot((q * q).astype(BF16), hm, preferred_element_type=F32)
    q_ref[...] = q * lax.rsqrt(q_ms + EPS) * (qg_ref[...] * (HEAD_DIM ** -0.5 * LOG2_E))
    k = proj(D_ATTN, D_ATTN)
    k_ms = jnp.dot((k * k).astype(BF16), hm, preferred_element_type=F32)
    k_ref[...] = k * lax.rsqrt(k_ms + EPS) * kg_ref[...]


def _ffn_inproj(x2, seq, n1, wg, wu, wd, nm, win, qg, kg, hm, cw, cb, cg, cbeta):
    T = x2.shape[0]
    tm = TM_FFN
    n_tiles = T // tm

    def lagged(width, lag):
        return pl.BlockSpec((tm, width), lambda i: (jnp.clip(i - lag, 0, n_tiles - 1), 0))
    cur = functools.partial(lagged, lag=0)
    return pl.pallas_call(
        functools.partial(_ffn_inproj_kernel, seq // tm),
        grid=(n_tiles + 1,),
        in_specs=[cur(D_MODEL), _const_spec((1, D_MODEL)),
                  _const_spec((D_MODEL, D_FF)), _const_spec((D_MODEL, D_FF)), _const_spec((D_FF, D_MODEL)),
                  _const_spec((1, D_MODEL)), _const_spec((D_MODEL, D_IN)),
                  _const_spec((1, D_ATTN)), _const_spec((1, D_ATTN)), _const_spec((D_ATTN, D_ATTN)),
                  _const_spec((CONV_K, D_CONV)), _const_spec((1, D_CONV)), _const_spec((1, D_CONV)),
                  _const_spec((1, D_CONV))],
        out_specs=[cur(D_MODEL), cur(D_ATTN), cur(D_ATTN), cur(D_ATTN), lagged(D_CONV, 1)],
        out_shape=[jax.ShapeDtypeStruct((T, D_MODEL), F32)] + [jax.ShapeDtypeStruct((T, D_ATTN), F32)] * 3
                  + [jax.ShapeDtypeStruct((T, D_CONV), BF16)],
        scratch_shapes=[pltpu.VMEM((tm + HALO, D_CONV), F32), pltpu.VMEM((tm, D_CONV), F32)],
        compiler_params=pltpu.CompilerParams(dimension_semantics=("arbitrary",),
                                             vmem_limit_bytes=VMEM_LIMIT_BYTES),
        name="ffn_inproj_conv",
    )(x2, n1, wg, wu, wd, nm, win, qg, kg, hm, cw, cb, cg, cbeta)


def _rows(ref, start, stride):
    if stride == 1:
        return ref[pl.ds(start, BLK), :]
    return ref[pl.ds(start, BLK, stride=stride), :]


def _store_rows(ref, start, stride, val):
    if stride == 1:
        ref[pl.ds(start, BLK), :] = val
    else:
        ref[pl.ds(start, BLK, stride=stride), :] = val


def _fill_band_mask(mask_ref):
    qi = lax.broadcasted_iota(jnp.int32, (2 * BLK, 2 * BLK), 0) & (BLK - 1)
    ci = lax.broadcasted_iota(jnp.int32, (2 * BLK, 2 * BLK), 1)
    dist = BLK + qi - ci
    mask_ref[...] = jnp.where((dist >= 0) & (dist <= BLK), 0.0, NEG)


def _attn_chain(src, dst, mask_ref, chain, has_prev, order):
    q_ref, k_ref, v_ref, src0, src_stride, src_step = src
    a_ref, m_ref, l_ref, dst0, dst_stride, dst_step = dst
    lane = lax.broadcasted_iota(jnp.int32, (BLK, PAIR), 1)
    first = lane < HEAD_DIM
    lo = -1 if has_prev else 0
    k_f = [_rows(k_ref, src0 + j * src_step, src_stride) for j in range(lo, chain)]
    v_f = [_rows(v_ref, src0 + j * src_step, src_stride) for j in range(lo, chain)]
    k_t = [k.astype(BF16) for k in k_f]
    v_t = [v.astype(BF16) for v in v_f]
    nt = (((1,), (1,)), ((), ()))
    accs = []
    for j in range(chain):
        q = _rows(q_ref, src0 + j * src_step, src_stride)
        if order is not None:
            q = jnp.concatenate([q[:SUBLANES] + order, q[SUBLANES:]], axis=0)
        row = dst0 + j * dst_step
        t = j - lo
        if t > 0:
            q2 = jnp.concatenate([jnp.where(first, q, 0.0), jnp.where(first, 0.0, q)], axis=0).astype(BF16)
            k_all = jnp.concatenate([k_t[t - 1], k_t[t]], axis=0)
            v_all = jnp.concatenate([v_t[t - 1], v_t[t]], axis=0)
            v_aug = jnp.concatenate([v_all, jnp.ones(v_all.shape, BF16)], axis=1)
            s = lax.dot_general(q2, k_all, nt, preferred_element_type=F32) + mask_ref[...]
            m = jnp.max(s, axis=-1, keepdims=True)
            p = jnp.exp2(s - m).astype(BF16)
            pv = jnp.dot(p, v_aug, preferred_element_type=F32)
            acc = jnp.where(first, pv[:BLK, :PAIR], pv[BLK:, :PAIR])
            den = jnp.where(first, pv[:BLK, PAIR:], pv[BLK:, PAIR:])
            top = jnp.where(first, m[:BLK], m[BLK:])
        else:
            one = jnp.where(first, 1.0, 0.0)
            k_all = jnp.concatenate([jnp.where(first, k_f[0], 0.0), jnp.where(first, 0.0, k_f[0])], axis=0)
            v_all = jnp.concatenate([jnp.where(first, v_f[0], 0.0), jnp.where(first, 0.0, v_f[0])], axis=0)
            v_aug = jnp.concatenate([v_all, jnp.concatenate([one, 1.0 - one], axis=0)], axis=1).astype(BF16)
            own = mask_ref[:BLK, BLK:]
            s = lax.dot_general(q.astype(BF16), k_all.astype(BF16), nt, preferred_element_type=F32)
            s_a, s_b = s[:, :BLK] + own, s[:, BLK:] + own
            m_a = jnp.max(s_a, axis=-1, keepdims=True)
            m_b = jnp.max(s_b, axis=-1, keepdims=True)
            p = jnp.concatenate([jnp.exp2(s_a - m_a), jnp.exp2(s_b - m_b)], axis=1).astype(BF16)
            pv = jnp.dot(p, v_aug, preferred_element_type=F32)
            acc, den, top = pv[:, :PAIR], pv[:, PAIR:], jnp.where(first, m_a, m_b)
        _store_rows(a_ref, row, dst_stride, acc)
        _store_rows(l_ref, row, dst_stride, den)
        _store_rows(m_ref, row, dst_stride, top)
        accs.append(acc)
    return _zero_after(jnp.concatenate(accs, axis=0))


def _attn_kernel(q_ref, k_ref, v_ref, out_ref, mask_ref, qp_ref, kp_ref, vp_ref, *stats):
    seq = q_ref.shape[0]
    _fill_band_mask(mask_ref)
    for src_ref, plane_ref in ((q_ref, qp_ref), (k_ref, kp_ref), (v_ref, vp_ref)):
        for p in range(ATTN_PLANES):
            for c in range(0, seq // ATTN_PLANES, BLK):
                plane_ref[p, c:c + BLK, :] = src_ref[pl.ds(ATTN_PLANES * c + p, BLK, stride=ATTN_PLANES), :]

    def chains_of(d, plane):
        dilation = DILATIONS[d]
        dst_refs = stats[3 * d:3 * d + 3]
        span = BLK * dilation
        n_blocks = seq // span
        chain = min(n_blocks, ATTN_GROUP)
        chains = []
        for r in range(dilation):
            if plane is not None and r % ATTN_PLANES != plane:
                continue
            for c in range(n_blocks // chain):
                start = c * chain * span + r
                if plane is None:
                    src = (q_ref, k_ref, v_ref, start, dilation, span)
                    dst = dst_refs + (start, dilation, span)
                else:
                    where = (start // ATTN_PLANES, dilation // ATTN_PLANES, span // ATTN_PLANES)
                    src = (qp_ref.at[plane], kp_ref.at[plane], vp_ref.at[plane]) + where
                    dst = tuple(ref.at[plane] for ref in dst_refs) + where
                chains.append(functools.partial(_attn_chain, src, dst, mask_ref, chain, c > 0))
        per_piece = ATTN_GROUP // chain

        def run(order, group):
            return functools.reduce(jnp.add, [one(order) for one in group])
        return [functools.partial(run, group=chains[g:g + per_piece]) for g in range(0, len(chains), per_piece)]

    def mix_piece(order, plane):
        per_plane = seq // ATTN_PLANES

        def rows_of(ref, dilation):
            if dilation % ATTN_PLANES:
                return ref[pl.ds(plane, per_plane, stride=ATTN_PLANES), :]
            return ref[plane]
        accs, ms, sums = ([rows_of(stats[3 * d + i], dil) for d, dil in enumerate(DILATIONS)] for i in range(3))
        if order is not None:
            ms[0] = ms[0] + order
        top = functools.reduce(jnp.maximum, ms)
        es = [jnp.exp2(m - top) for m in ms]
        num = functools.reduce(jnp.add, [e * a for e, a in zip(es, accs)])
        den = functools.reduce(jnp.add, [e * l for e, l in zip(es, sums)])
        out = num / den
        out_ref[pl.ds(plane, per_plane, stride=ATTN_PLANES), :] = out
        return _zero_after(out)

    pieces = [piece for d, dil in enumerate(DILATIONS) if dil % ATTN_PLANES for piece in chains_of(d, None)]
    for plane in range(ATTN_PLANES):
        for d, dil in enumerate(DILATIONS):
            if dil % ATTN_PLANES == 0:
                pieces += chains_of(d, plane)
        pieces.append(functools.partial(mix_piece, plane=plane))
    orders = []
    for piece in pieces:
        orders.append(piece(orders[-ATTN_IN_FLIGHT] if len(orders) >= ATTN_IN_FLIGHT else None))


def _dilated_attn(q, k, v):
    B, S, _ = q.shape
    spec = pl.BlockSpec((None, S, PAIR), lambda b, h: (b, 0, h))
    planes = pltpu.VMEM((ATTN_PLANES, S // ATTN_PLANES, PAIR), F32)
    return pl.pallas_call(
        _attn_kernel,
        grid=(B, N_PAIRS),
        in_specs=[spec, spec, spec],
        out_specs=spec,
        out_shape=jax.ShapeDtypeStruct((B, S, D_ATTN), F32),
        scratch_shapes=[pltpu.VMEM((2 * BLK, 2 * BLK), F32)] + [planes] * 3
                       + [pltpu.VMEM((S, PAIR), F32) if d % ATTN_PLANES else planes
                          for d in DILATIONS for _ in range(3)],
        compiler_params=pltpu.CompilerParams(dimension_semantics=("arbitrary", "arbitrary"),
                                             vmem_limit_bytes=VMEM_LIMIT_BYTES),
        name="dilated_attn",
    )(q, k, v)


def _outproj_ffn_kernel(h1_ref, attn_ref, conv_ref, woa_ref, woc_ref, n2_ref, wg_ref, wu_ref, wd_ref,
                        out_ref):
    mix = (jnp.dot(attn_ref[...].astype(BF16), woa_ref[...], preferred_element_type=F32)
           + jnp.dot(conv_ref[...], woc_ref[...], preferred_element_type=F32))
    h2 = h1_ref[...] + mix
    out_ref[...] = h2 + 0.5 * _swiglu(_rms(h2, n2_ref[...]).astype(BF16), wg_ref, wu_ref, wd_ref)


def _outproj_ffn(h1, attn, conv, woa, woc, n2, wg, wu, wd):
    T = h1.shape[0]
    tm = TM_OUT
    row = lambda w: pl.BlockSpec((tm, w), lambda i: (i, 0))
    return pl.pallas_call(
        _outproj_ffn_kernel,
        grid=(T // tm,),
        in_specs=[row(D_MODEL), row(D_ATTN), row(D_CONV),
                  _const_spec((D_ATTN, D_MODEL)), _const_spec((D_CONV, D_MODEL)), _const_spec((1, D_MODEL)),
                  _const_spec((D_MODEL, D_FF)), _const_spec((D_MODEL, D_FF)), _const_spec((D_FF, D_MODEL))],
        out_specs=row(D_MODEL),
        out_shape=jax.ShapeDtypeStruct((T, D_MODEL), F32),
        compiler_params=pltpu.CompilerParams(dimension_semantics=("arbitrary",),
                                             vmem_limit_bytes=VMEM_LIMIT_BYTES),
        name="outproj_ffn",
    )(h1, attn, conv, woa, woc, n2, wg, wu, wd)


def _layer(h, ffn1_norm, ffn1_w_gate, ffn1_w_up, ffn1_w_down, mix_norm, w_in, q_norm, k_norm, conv_w,
           conv_b, conv_ln_g, conv_ln_b, w_out, ffn2_norm, ffn2_w_gate, ffn2_w_up, ffn2_w_down):
    B, S, D = h.shape
    T = B * S
    row = lambda a: a.reshape(1, -1).astype(F32)
    head_id = jnp.arange(D_ATTN) // HEAD_DIM
    head_mean = jnp.where(head_id[:, None] == head_id[None, :], 1.0 / HEAD_DIM, 0.0).astype(BF16)

    h1, q, k, v, conv = _ffn_inproj(
        h.reshape(T, D), S, row(ffn1_norm), ffn1_w_gate.astype(BF16), ffn1_w_up.astype(BF16),
        ffn1_w_down.astype(BF16), row(mix_norm), w_in.astype(BF16),
        row(jnp.tile(q_norm, HEADS)), row(jnp.tile(k_norm, HEADS)), head_mean,
        conv_w.astype(F32), row(conv_b), row(conv_ln_g), row(conv_ln_b))

    attn = _dilated_attn(q.reshape(B, S, D_ATTN), k.reshape(B, S, D_ATTN), v.reshape(B, S, D_ATTN))

    w_out16 = w_out.astype(BF16)
    out = _outproj_ffn(h1, attn.reshape(T, D_ATTN), conv, w_out16[:D_ATTN], w_out16[D_ATTN:],
                       row(ffn2_norm), ffn2_w_gate.astype(BF16), ffn2_w_up.astype(BF16),
                       ffn2_w_down.astype(BF16))
    return out.reshape(B, S, D)


def kernel(x, ffn1_norm, ffn1_w_gate, ffn1_w_up, ffn1_w_down, mix_norm, w_in, q_norm, k_norm, conv_w, conv_b,
           conv_ln_g, conv_ln_b, w_out, ffn2_norm, ffn2_w_gate, ffn2_w_up, ffn2_w_down):
    assert x.shape[1] % (BLK * max(DILATIONS)) == 0 and x.shape[1] % TM_FFN == 0 and x.shape[2] == D_MODEL
    h = x
    for l in range(ffn1_norm.shape[0]):
        h = _layer(h, ffn1_norm[l], ffn1_w_gate[l], ffn1_w_up[l], ffn1_w_down[l], mix_norm[l], w_in[l],
                   q_norm[l], k_norm[l], conv_w[l], conv_b[l], conv_ln_g[l], conv_ln_b[l], w_out[l],
                   ffn2_norm[l], ffn2_w_gate[l], ffn2_w_up[l], ffn2_w_down[l])
    return h
```

```python
import functools

import jax
import jax.numpy as jnp
from jax import lax
from jax.experimental import pallas as pl
from jax.experimental.pallas import tpu as pltpu

F32 = jnp.float32
BF16 = jnp.bfloat16

D_MODEL = 1024
D_FF = 2816
HEADS = 8
HEAD_DIM = 64
D_ATTN = HEADS * HEAD_DIM
D_CONV = 512
D_IN = 3 * D_ATTN + 2 * D_CONV
CONV_K = 31
BLK = 128
DILATIONS = (1, 4, 16)
EPS = 1e-6
NEG = -1e30
LOG2_E = 1.4426950408889634

LANES = 128
SUBLANES = 8
PAIR = 2 * HEAD_DIM
N_PAIRS = D_ATTN // PAIR
HALO = 32

VMEM_LIMIT_BYTES = 56 * 1024 * 1024

TM_FFN = 512
TM_OUT = 1024
FF_CHUNK = 256
CONV_CHUNK = 64
CONV_SLABS = D_CONV // LANES
ATTN_GROUP = 4
ATTN_PLANES = 4
ATTN_IN_FLIGHT = 6


def _const_spec(shape):
    return pl.BlockSpec(shape, lambda *_: (0,) * len(shape), pipeline_mode=pl.Buffered(1))


def _rms(x, g):
    ms = jnp.mean(x * x, axis=-1, keepdims=True)
    return x * lax.rsqrt(ms + EPS) * g


def _swiglu(xn, wg_ref, wu_ref, wd_ref, side_work=None):
    def after(piece, operand):
        order = side_work(piece) if side_work is not None else None
        if order is None:
            return operand
        head = 2 * SUBLANES
        order = jnp.concatenate([order] * (operand.shape[1] // LANES), axis=1).astype(operand.dtype)
        return jnp.concatenate([operand[:head] + order, operand[head:]], axis=0)

    acc = None
    for c in range(D_FF // FF_CHUNK):
        cols = slice(c * FF_CHUNK, (c + 1) * FF_CHUNK)
        g = jnp.dot(after(3 * c - 1, xn), wg_ref[:, cols], preferred_element_type=F32)
        u = jnp.dot(after(3 * c, xn), wu_ref[:, cols], preferred_element_type=F32)
        a = after(3 * c + 1, g * jax.nn.sigmoid(g) * u)
        part = jnp.dot(a.astype(BF16), wd_ref[cols, :], preferred_element_type=F32)
        acc = part if acc is None else acc + part
    return acc


def _zero_after(x):
    bits = pltpu.bitcast(x, jnp.uint32)
    fold = bits[0:SUBLANES, :]
    for r in range(SUBLANES, x.shape[0], SUBLANES):
        fold = fold | bits[r:r + SUBLANES, :]
    zero = lax.shift_right_logical(lax.shift_right_logical(fold, jnp.uint32(16)), jnp.uint32(16))
    return pltpu.bitcast(zero, F32)[0:1, :]


def _conv_taps(pad_ref, w_ref, row0, lanes, order):
    first = HALO - (CONV_K - 1)
    y = None
    for b in range(SUBLANES):
        rows = CONV_CHUNK + (SUBLANES if b else 0)
        u = None
        for a in range(HALO // SUBLANES + 1):
            j = SUBLANES * a + b - first
            if 0 <= j < CONV_K:
                lo = row0 + SUBLANES * a
                term = pad_ref[lo:lo + rows, lanes] * (w_ref[j:j + 1, lanes] + order)
                u = term if u is None else u + term
        u = u[b:b + CONV_CHUNK, :] if b else u
        y = u if y is None else y + u
    return y


def _conv_piece(piece, order, pad_ref, y_ref, w_ref, b_ref, g_ref, beta_ref, out_ref):
    c, s = divmod(piece, CONV_SLABS)
    rows = slice(c * CONV_CHUNK, (c + 1) * CONV_CHUNK)
    lanes = slice(s * LANES, (s + 1) * LANES)
    y = _conv_taps(pad_ref, w_ref, c * CONV_CHUNK, lanes, order)
    y_ref[rows, lanes] = y + b_ref[:, lanes]
    if s == CONV_SLABS - 1:
        yr = y_ref[rows, :]
        mu = jnp.mean(yr, axis=-1, keepdims=True)
        yc = yr - mu
        var = jnp.mean(yc * yc, axis=-1, keepdims=True)
        z = yc * lax.rsqrt(var + EPS) * g_ref[...] + beta_ref[...]
        out_ref[rows, :] = (z * jax.nn.sigmoid(z)).astype(out_ref.dtype)
    return _zero_after(y)


def _ffn_inproj_kernel(tiles_per_seq, x_ref, n1_ref, wg_ref, wu_ref, wd_ref, nm_ref, win_ref, qg_ref,
                       kg_ref, hm_ref, cw_ref, cb_ref, cg_ref, cbeta_ref,
                       h1_ref, q_ref, k_ref, v_ref, conv_ref, pad_ref, y_ref):
    i = pl.program_id(0)
    tm = x_ref.shape[0]

    @pl.when(i == 0)
    def _():
        pad_ref[...] = jnp.zeros(pad_ref.shape, F32)

    n_pieces = (tm // CONV_CHUNK) * CONV_SLABS
    assert n_pieces <= 3 * (D_FF // FF_CHUNK) - 1
    order = [jnp.zeros((1, LANES), F32)]

    def conv_side_work(piece):
        if not 0 <= piece < n_pieces:
            return None
        order[0] = _conv_piece(piece, order[0], pad_ref, y_ref, cw_ref, cb_ref, cg_ref, cbeta_ref, conv_ref)
        return order[0]

    x = x_ref[...]
    h1 = x + 0.5 * _swiglu(_rms(x, n1_ref[...]).astype(BF16), wg_ref, wu_ref, wd_ref, conv_side_work)
    h1_ref[...] = h1

    hn = _rms(h1, nm_ref[...]).astype(BF16)

    def proj(col, width):
        return jnp.dot(hn, win_ref[:, col:col + width], preferred_element_type=F32)

    cg = proj(3 * D_ATTN + D_CONV, D_CONV)
    ca = proj(3 * D_ATTN, D_CONV)
    same_seq = (i % tiles_per_seq) != 0
    pad_ref[0:HALO, :] = jnp.where(same_seq, pad_ref[tm:tm + HALO, :], 0.0)
    pad_ref[HALO:, :] = ca * jax.nn.sigmoid(cg)

    v_ref[...] = proj(2 * D_ATTN, D_ATTN)

    hm = hm_ref[...]
    q = proj(0, D_ATTN)
    q_ms = jnp.dot((q * q).astype(BF16), hm, preferred_element_type=F32)
    q_ref[...] = q * lax.rsqrt(q_ms + EPS) * (qg_ref[...] * (HEAD_DIM ** -0.5 * LOG2_E))
    k = proj(D_ATTN, D_ATTN)
    k_ms = jnp.dot((k * k).astype(BF16), hm, preferred_element_type=F32)
    k_ref[...] = k * lax.rsqrt(k_ms + EPS) * kg_ref[...]


def _ffn_inproj(x2, seq, n1, wg, wu, wd, nm, win, qg, kg, hm, cw, cb, cg, cbeta):
    T = x2.shape[0]
    tm = TM_FFN
    n_tiles = T // tm

    def lagged(width, lag):
        return pl.BlockSpec((tm, width), lambda i: (jnp.clip(i - lag, 0, n_tiles - 1), 0))
    cur = functools.partial(lagged, lag=0)
    return pl.pallas_call(
        functools.partial(_ffn_inproj_kernel, seq // tm),
        grid=(n_tiles + 1,),
        in_specs=[cur(D_MODEL), _const_spec((1, D_MODEL)),
                  _const_spec((D_MODEL, D_FF)), _const_spec((D_MODEL, D_FF)), _const_spec((D_FF, D_MODEL)),
                  _const_spec((1, D_MODEL)), _const_spec((D_MODEL, D_IN)),
                  _const_spec((1, D_ATTN)), _const_spec((1, D_ATTN)), _const_spec((D_ATTN, D_ATTN)),
                  _const_spec((CONV_K, D_CONV)), _const_spec((1, D_CONV)), _const_spec((1, D_CONV)),
                  _const_spec((1, D_CONV))],
        out_specs=[cur(D_MODEL), cur(D_ATTN), cur(D_ATTN), cur(D_ATTN), lagged(D_CONV, 1)],
        out_shape=[jax.ShapeDtypeStruct((T, D_MODEL), F32)] + [jax.ShapeDtypeStruct((T, D_ATTN), F32)] * 3
                  + [jax.ShapeDtypeStruct((T, D_CONV), BF16)],
        scratch_shapes=[pltpu.VMEM((tm + HALO, D_CONV), F32), pltpu.VMEM((tm, D_CONV), F32)],
        compiler_params=pltpu.CompilerParams(dimension_semantics=("arbitrary",),
                                             vmem_limit_bytes=VMEM_LIMIT_BYTES),
        name="ffn_inproj_conv",
    )(x2, n1, wg, wu, wd, nm, win, qg, kg, hm, cw, cb, cg, cbeta)


def _rows(ref, start, stride):
    if stride == 1:
        return ref[pl.ds(start, BLK), :]
    return ref[pl.ds(start, BLK, stride=stride), :]


def _store_rows(ref, start, stride, val):
    if stride == 1:
        ref[pl.ds(start, BLK), :] = val
    else:
        ref[pl.ds(start, BLK, stride=stride), :] = val


def _fill_band_mask(mask_ref):
    qi = lax.broadcasted_iota(jnp.int32, (2 * BLK, 2 * BLK), 0) & (BLK - 1)
    ci = lax.broadcasted_iota(jnp.int32, (2 * BLK, 2 * BLK), 1)
    dist = BLK + qi - ci
    mask_ref[...] = jnp.where((dist >= 0) & (dist <= BLK), 0.0, NEG)


def _attn_chain(src, dst, mask_ref, chain, has_prev, order):
    q_ref, k_ref, v_ref, src0, src_stride, src_step = src
    a_ref, m_ref, l_ref, dst0, dst_stride, dst_step = dst
    lane = lax.broadcasted_iota(jnp.int32, (BLK, PAIR), 1)
    first = lane < HEAD_DIM
    lo = -1 if has_prev else 0
    k_f = [_rows(k_ref, src0 + j * src_step, src_stride) for j in range(lo, chain)]
    v_f = [_rows(v_ref, src0 + j * src_step, src_stride) for j in range(lo, chain)]
    k_t = [k.astype(BF16) for k in k_f]
    v_t = [v.astype(BF16) for v in v_f]
    nt = (((1,), (1,)), ((), ()))
    accs = []
    for j in range(chain):
        q = _rows(q_ref, src0 + j * src_step, src_stride)
        if order is not None:
            q = jnp.concatenate([q[:SUBLANES] + order, q[SUBLANES:]], axis=0)
        row = dst0 + j * dst_step
        t = j - lo
        if t > 0:
            q2 = jnp.concatenate([jnp.where(first, q, 0.0), jnp.where(first, 0.0, q)], axis=0).astype(BF16)
            k_all = jnp.concatenate([k_t[t - 1], k_t[t]], axis=0)
            v_all = jnp.concatenate([v_t[t - 1], v_t[t]], axis=0)
            v_aug = jnp.concatenate([v_all, jnp.ones(v_all.shape, BF16)], axis=1)
            s = lax.dot_general(q2, k_all, nt, preferred_element_type=F32) + mask_ref[...]
            m = jnp.max(s, axis=-1, keepdims=True)
            p = jnp.exp2(s - m).astype(BF16)
            pv = jnp.dot(p, v_aug, preferred_element_type=F32)
            acc = jnp.where(first, pv[:BLK, :PAIR], pv[BLK:, :PAIR])
            den = jnp.where(first, pv[:BLK, PAIR:], pv[BLK:, PAIR:])
            top = jnp.where(first, m[:BLK], m[BLK:])
        else:
            one = jnp.where(first, 1.0, 0.0)
            k_all = jnp.concatenate([jnp.where(first, k_f[0], 0.0), jnp.where(first, 0.0, k_f[0])], axis=0)
            v_all = jnp.concatenate([jnp.where(first, v_f[0], 0.0), jnp.where(first, 0.0, v_f[0])], axis=0)
            v_aug = jnp.concatenate([v_all, jnp.concatenate([one, 1.0 - one], axis=0)], axis=1).astype(BF16)
            own = mask_ref[:BLK, BLK:]
            s = lax.dot_general(q.astype(BF16), k_all.astype(BF16), nt, preferred_element_type=F32)
            s_a, s_b = s[:, :BLK] + own, s[:, BLK:] + own
            m_a = jnp.max(s_a, axis=-1, keepdims=True)
            m_b = jnp.max(s_b, axis=-1, keepdims=True)
            p = jnp.concatenate([jnp.exp2(s_a - m_a), jnp.exp2(s_b - m_b)], axis=1).astype(BF16)
            pv = jnp.dot(p, v_aug, preferred_element_type=F32)
            acc, den, top = pv[:, :PAIR], pv[:, PAIR:], jnp.where(first, m_a, m_b)
        _store_rows(a_ref, row, dst_stride, acc)
        _store_rows(l_ref, row, dst_stride, den)
        _store_rows(m_ref, row, dst_stride, top)
        accs.append(acc)
    return _zero_after(jnp.concatenate(accs, axis=0))


def _attn_kernel(q_ref, k_ref, v_ref, out_ref, mask_ref, qp_ref, kp_ref, vp_ref, *stats):
    seq = q_ref.shape[0]
    _fill_band_mask(mask_ref)
    for src_ref, plane_ref in ((q_ref, qp_ref), (k_ref, kp_ref), (v_ref, vp_ref)):
        for p in range(ATTN_PLANES):
            for c in range(0, seq // ATTN_PLANES, BLK):
                plane_ref[p, c:c + BLK, :] = src_ref[pl.ds(ATTN_PLANES * c + p, BLK, stride=ATTN_PLANES), :]

    def chains_of(d, plane):
        dilation = DILATIONS[d]
        dst_refs = stats[3 * d:3 * d + 3]
        span = BLK * dilation
        n_blocks = seq // span
        chain = min(n_blocks, ATTN_GROUP)
        chains = []
        for r in range(dilation):
            if plane is not None and r % ATTN_PLANES != plane:
                continue
            for c in range(n_blocks // chain):
                start = c * chain * span + r
                if plane is None:
                    src = (q_ref, k_ref, v_ref, start, dilation, span)
                    dst = dst_refs + (start, dilation, span)
                else:
                    where = (start // ATTN_PLANES, dilation // ATTN_PLANES, span // ATTN_PLANES)
                    src = (qp_ref.at[plane], kp_ref.at[plane], vp_ref.at[plane]) + where
                    dst = tuple(ref.at[plane] for ref in dst_refs) + where
                chains.append(functools.partial(_attn_chain, src, dst, mask_ref, chain, c > 0))
        per_piece = ATTN_GROUP // chain

        def run(order, group):
            return functools.reduce(jnp.add, [one(order) for one in group])
        return [functools.partial(run, group=chains[g:g + per_piece]) for g in range(0, len(chains), per_piece)]

    def mix_piece(order, plane):
        per_plane = seq // ATTN_PLANES

        def rows_of(ref, dilation):
            if dilation % ATTN_PLANES:
                return ref[pl.ds(plane, per_plane, stride=ATTN_PLANES), :]
            return ref[plane]
        accs, ms, sums = ([rows_of(stats[3 * d + i], dil) for d, dil in enumerate(DILATIONS)] for i in range(3))
        if order is not None:
            ms[0] = ms[0] + order
        top = functools.reduce(jnp.maximum, ms)
        es = [jnp.exp2(m - top) for m in ms]
        num = functools.reduce(jnp.add, [e * a for e, a in zip(es, accs)])
        den = functools.reduce(jnp.add, [e * l for e, l in zip(es, sums)])
        out = num / den
        out_ref[pl.ds(plane, per_plane, stride=ATTN_PLANES), :] = out
        return _zero_after(out)

    natural = [piece for d, dil in enumerate(DILATIONS) if dil % ATTN_PLANES for piece in chains_of(d, None)]
    planes = [[piece for d, dil in enumerate(DILATIONS) if dil % ATTN_PLANES == 0 for piece in chains_of(d, p)]
              for p in range(ATTN_PLANES)]
    share = -(-len(natural) // (ATTN_PLANES // 2))
    pieces, mixes = [], []
    for p in range(ATTN_PLANES):
        pieces += natural[p * share:(p + 1) * share] + planes[p]
        mixes.append(functools.partial(mix_piece, plane=p))
        if (p + 1) * share >= len(natural):
            pieces += mixes
            mixes = []
    orders = []
    for piece in pieces:
        orders.append(piece(orders[-ATTN_IN_FLIGHT] if len(orders) >= ATTN_IN_FLIGHT else None))


def _dilated_attn(q, k, v):
    B, S, _ = q.shape
    spec = pl.BlockSpec((None, S, PAIR), lambda b, h: (b, 0, h))
    planes = pltpu.VMEM((ATTN_PLANES, S // ATTN_PLANES, PAIR), F32)
    return pl.pallas_call(
        _attn_kernel,
        grid=(B, N_PAIRS),
        in_specs=[spec, spec, spec],
        out_specs=spec,
        out_shape=jax.ShapeDtypeStruct((B, S, D_ATTN), F32),
        scratch_shapes=[pltpu.VMEM((2 * BLK, 2 * BLK), F32)] + [planes] * 3
                       + [pltpu.VMEM((S, PAIR), F32) if d % ATTN_PLANES else planes
                          for d in DILATIONS for _ in range(3)],
        compiler_params=pltpu.CompilerParams(dimension_semantics=("arbitrary", "arbitrary"),
                                             vmem_limit_bytes=VMEM_LIMIT_BYTES),
        name="dilated_attn",
    )(q, k, v)


def _outproj_ffn_kernel(h1_ref, attn_ref, conv_ref, woa_ref, woc_ref, n2_ref, wg_ref, wu_ref, wd_ref,
                        out_ref):
    mix = (jnp.dot(attn_ref[...].astype(BF16), woa_ref[...], preferred_element_type=F32)
           + jnp.dot(conv_ref[...], woc_ref[...], preferred_element_type=F32))
    h2 = h1_ref[...] + mix
    out_ref[...] = h2 + 0.5 * _swiglu(_rms(h2, n2_ref[...]).astype(BF16), wg_ref, wu_ref, wd_ref)


def _outproj_ffn(h1, attn, conv, woa, woc, n2, wg, wu, wd):
    T = h1.shape[0]
    tm = TM_OUT
    row = lambda w: pl.BlockSpec((tm, w), lambda i: (i, 0))
    return pl.pallas_call(
        _outproj_ffn_kernel,
        grid=(T // tm,),
        in_specs=[row(D_MODEL), row(D_ATTN), row(D_CONV),
                  _const_spec((D_ATTN, D_MODEL)), _const_spec((D_CONV, D_MODEL)), _const_spec((1, D_MODEL)),
                  _const_spec((D_MODEL, D_FF)), _const_spec((D_MODEL, D_FF)), _const_spec((D_FF, D_MODEL))],
        out_specs=row(D_MODEL),
        out_shape=jax.ShapeDtypeStruct((T, D_MODEL), F32),
        compiler_params=pltpu.CompilerParams(dimension_semantics=("arbitrary",),
                                             vmem_limit_bytes=VMEM_LIMIT_BYTES),
        name="outproj_ffn",
    )(h1, attn, conv, woa, woc, n2, wg, wu, wd)


def _layer(h, ffn1_norm, ffn1_w_gate, ffn1_w_up, ffn1_w_down, mix_norm, w_in, q_norm, k_norm, conv_w,
           conv_b, conv_ln_g, conv_ln_b, w_out, ffn2_norm, ffn2_w_gate, ffn2_w_up, ffn2_w_down):
    B, S, D = h.shape
    T = B * S
    row = lambda a: a.reshape(1, -1).astype(F32)
    head_id = jnp.arange(D_ATTN) // HEAD_DIM
    head_mean = jnp.where(head_id[:, None] == head_id[None, :], 1.0 / HEAD_DIM, 0.0).astype(BF16)

    h1, q, k, v, conv = _ffn_inproj(
        h.reshape(T, D), S, row(ffn1_norm), ffn1_w_gate.astype(BF16), ffn1_w_up.astype(BF16),
        ffn1_w_down.astype(BF16), row(mix_norm), w_in.astype(BF16),
        row(jnp.tile(q_norm, HEADS)), row(jnp.tile(k_norm, HEADS)), head_mean,
        conv_w.astype(F32), row(conv_b), row(conv_ln_g), row(conv_ln_b))

    attn = _dilated_attn(q.reshape(B, S, D_ATTN), k.reshape(B, S, D_ATTN), v.reshape(B, S, D_ATTN))

    w_out16 = w_out.astype(BF16)
    out = _outproj_ffn(h1, attn.reshape(T, D_ATTN), conv, w_out16[:D_ATTN], w_out16[D_ATTN:],
                       row(ffn2_norm), ffn2_w_gate.astype(BF16), ffn2_w_up.astype(BF16),
                       ffn2_w_down.astype(BF16))
    return out.reshape(B, S, D)


def kernel(x, ffn1_norm, ffn1_w_gate, ffn1_w_up, ffn1_w_down, mix_norm, w_in, q_norm, k_norm, conv_w, conv_b,
           conv_ln_g, conv_ln_b, w_out, ffn2_norm, ffn2_w_gate, ffn2_w_up, ffn2_w_down):
    assert x.shape[1] % (BLK * max(DILATIONS)) == 0 and x.shape[1] % TM_FFN == 0 and x.shape[2] == D_MODEL
    h = x
    for l in range(ffn1_norm.shape[0]):
        h = _layer(h, ffn1_norm[l], ffn1_w_gate[l], ffn1_w_up[l], ffn1_w_down[l], mix_norm[l], w_in[l],
                   q_norm[l], k_norm[l], conv_w[l], conv_b[l], conv_ln_g[l], conv_ln_b[l], w_out[l],
                   ffn2_norm[l], ffn2_w_gate[l], ffn2_w_up[l], ffn2_w_down[l])
    return h
```

```python
import functools

import jax
import jax.numpy as jnp
from jax import lax
from jax.experimental import pallas as pl
from jax.experimental.pallas import tpu as pltpu

F32 = jnp.float32
BF16 = jnp.bfloat16

D_MODEL = 1024
D_FF = 2816
HEADS = 8
HEAD_DIM = 64
D_ATTN = HEADS * HEAD_DIM
D_CONV = 512
D_IN = 3 * D_ATTN + 2 * D_CONV
CONV_K = 31
BLK = 128
DILATIONS = (1, 4, 16)
EPS = 1e-6
NEG = -1e30
LOG2_E = 1.4426950408889634

LANES = 128
SUBLANES = 8
PAIR = 2 * HEAD_DIM
N_PAIRS = D_ATTN // PAIR
HALO = 32

VMEM_LIMIT_BYTES = 56 * 1024 * 1024

TM_FFN = 512
TM_OUT = 512
FF_CHUNK = 256
CONV_CHUNK = 64
CONV_SLABS = D_CONV // LANES
ATTN_GROUP = 4
ATTN_PLANES = 4
ATTN_IN_FLIGHT = 9


def _const_spec(shape):
    return pl.BlockSpec(shape, lambda *_: (0,) * len(shape), pipeline_mode=pl.Buffered(1))


def _rms(x, g):
    ms = jnp.mean(x * x, axis=-1, keepdims=True)
    return x * lax.rsqrt(ms + EPS) * g


def _swiglu(xn, wg_ref, wu_ref, wd_ref, side_work=None):
    def after(piece, operand):
        order = side_work(piece) if side_work is not None else None
        if order is None:
            return operand
        head = 2 * SUBLANES
        order = jnp.concatenate([order] * (operand.shape[1] // LANES), axis=1).astype(operand.dtype)
        return jnp.concatenate([operand[:head] + order, operand[head:]], axis=0)

    acc = None
    for c in range(D_FF // FF_CHUNK):
        cols = slice(c * FF_CHUNK, (c + 1) * FF_CHUNK)
        g = jnp.dot(after(3 * c - 1, xn), wg_ref[:, cols], preferred_element_type=F32)
        u = jnp.dot(after(3 * c, xn), wu_ref[:, cols], preferred_element_type=F32)
        a = after(3 * c + 1, g * jax.nn.sigmoid(g) * u)
        part = jnp.dot(a.astype(BF16), wd_ref[cols, :].astype(BF16), preferred_element_type=F32)
        acc = part if acc is None else acc + part
    return acc


def _zero_after(x):
    bits = pltpu.bitcast(x, jnp.uint32)
    fold = bits[0:SUBLANES, :]
    for r in range(SUBLANES, x.shape[0], SUBLANES):
        fold = fold | bits[r:r + SUBLANES, :]
    zero = lax.shift_right_logical(lax.shift_right_logical(fold, jnp.uint32(16)), jnp.uint32(16))
    return pltpu.bitcast(zero, F32)[0:1, :]


def _conv_taps(pad_ref, w_ref, row0, lanes, order):
    first = HALO - (CONV_K - 1)
    y = None
    for b in range(SUBLANES):
        rows = CONV_CHUNK + (SUBLANES if b else 0)
        u = None
        for a in range(HALO // SUBLANES + 1):
            j = SUBLANES * a + b - first
            if 0 <= j < CONV_K:
                lo = row0 + SUBLANES * a
                term = pad_ref[lo:lo + rows, lanes] * (w_ref[j:j + 1, lanes] + order)
                u = term if u is None else u + term
        u = u[b:b + CONV_CHUNK, :] if b else u
        y = u if y is None else y + u
    return y


def _conv_piece(piece, order, pad_ref, y_ref, w_ref, b_ref, g_ref, beta_ref, out_ref):
    c, s = divmod(piece, CONV_SLABS)
    rows = slice(c * CONV_CHUNK, (c + 1) * CONV_CHUNK)
    lanes = slice(s * LANES, (s + 1) * LANES)
    y = _conv_taps(pad_ref, w_ref, c * CONV_CHUNK, lanes, order)
    y_ref[rows, lanes] = y + b_ref[:, lanes]
    if s == CONV_SLABS - 1:
        yr = y_ref[rows, :]
        mu = jnp.mean(yr, axis=-1, keepdims=True)
        yc = yr - mu
        var = jnp.mean(yc * yc, axis=-1, keepdims=True)
        z = yc * lax.rsqrt(var + EPS) * g_ref[...] + beta_ref[...]
        out_ref[rows, :] = (z * jax.nn.sigmoid(z)).astype(out_ref.dtype)
    return _zero_after(y)


def _ffn_inproj_kernel(tiles_per_seq, x_ref, n1_ref, wg_ref, wu_ref, wd_ref, nm_ref, win_ref, qg_ref,
                       kg_ref, hm_ref, cw_ref, cb_ref, cg_ref, cbeta_ref,
                       h1_ref, q_ref, k_ref, v_ref, conv_ref, pad_ref, y_ref):
    i = pl.program_id(0)
    tm = x_ref.shape[0]

    @pl.when(i == 0)
    def _():
        pad_ref[...] = jnp.zeros(pad_ref.shape, F32)

    n_pieces = (tm // CONV_CHUNK) * CONV_SLABS
    assert n_pieces <= 3 * (D_FF // FF_CHUNK) - 1
    order = [jnp.zeros((1, LANES), F32)]

    def conv_side_work(piece):
        if not 0 <= piece < n_pieces:
            return None
        order[0] = _conv_piece(piece, order[0], pad_ref, y_ref, cw_ref, cb_ref, cg_ref, cbeta_ref, conv_ref)
        return order[0]

    x = x_ref[...]
    h1 = x + 0.5 * _swiglu(_rms(x, n1_ref[...]).astype(BF16), wg_ref, wu_ref, wd_ref, conv_side_work)
    h1_ref[...] = h1

    hn = _rms(h1, nm_ref[...]).astype(BF16)

    def proj(col, width):
        return jnp.dot(hn, win_ref[:, col:col + width], preferred_element_type=F32)

    cg = proj(3 * D_ATTN + D_CONV, D_CONV)
    ca = proj(3 * D_ATTN, D_CONV)
    same_seq = (i % tiles_per_seq) != 0
    pad_ref[0:HALO, :] = jnp.where(same_seq, pad_ref[tm:tm + HALO, :], 0.0)
    pad_ref[HALO:, :] = ca * jax.nn.sigmoid(cg)

    v_ref[...] = proj(2 * D_ATTN, D_ATTN)

    hm = hm_ref[...]
    q = proj(0, D_ATTN)
    q_ms = jnp.dot((q * q).astype(BF16), hm, preferred_element_type=F32)
    q_ref[...] = q * lax.rsqrt(q_ms + EPS) * (qg_ref[...] * (HEAD_DIM ** -0.5 * LOG2_E))
    k = proj(D_ATTN, D_ATTN)
    k_ms = jnp.dot((k * k).astype(BF16), hm, preferred_element_type=F32)
    k_ref[...] = k * lax.rsqrt(k_ms + EPS) * kg_ref[...]


def _ffn_inproj(x2, seq, n1, wg, wu, wd, nm, win, qg, kg, hm, cw, cb, cg, cbeta):
    T = x2.shape[0]
    tm = TM_FFN
    n_tiles = T // tm

    def lagged(width, lag):
        return pl.BlockSpec((tm, width), lambda i: (jnp.clip(i - lag, 0, n_tiles - 1), 0))
    cur = functools.partial(lagged, lag=0)
    return pl.pallas_call(
        functools.partial(_ffn_inproj_kernel, seq // tm),
        grid=(n_tiles + 1,),
        in_specs=[cur(D_MODEL), _const_spec((1, D_MODEL)),
                  _const_spec((D_MODEL, D_FF)), _const_spec((D_MODEL, D_FF)), _const_spec((D_FF, D_MODEL)),
                  _const_spec((1, D_MODEL)), _const_spec((D_MODEL, D_IN)),
                  _const_spec((1, D_ATTN)), _const_spec((1, D_ATTN)), _const_spec((D_ATTN, D_ATTN)),
                  _const_spec((CONV_K, D_CONV)), _const_spec((1, D_CONV)), _const_spec((1, D_CONV)),
                  _const_spec((1, D_CONV))],
        out_specs=[cur(D_MODEL), cur(D_ATTN), cur(D_ATTN), cur(D_ATTN), lagged(D_CONV, 1)],
        out_shape=[jax.ShapeDtypeStruct((T, D_MODEL), F32)] + [jax.ShapeDtypeStruct((T, D_ATTN), F32)] * 3
                  + [jax.ShapeDtypeStruct((T, D_CONV), BF16)],
        scratch_shapes=[pltpu.VMEM((tm + HALO, D_CONV), F32), pltpu.VMEM((tm, D_CONV), F32)],
        compiler_params=pltpu.CompilerParams(dimension_semantics=("arbitrary",),
                                             vmem_limit_bytes=VMEM_LIMIT_BYTES),
        name="ffn_inproj_conv",
    )(x2, n1, wg, wu, wd, nm, win, qg, kg, hm, cw, cb, cg, cbeta)


def _rows(ref, start, stride):
    if stride == 1:
        return ref[pl.ds(start, BLK), :]
    return ref[pl.ds(start, BLK, stride=stride), :]


def _store_rows(ref, start, stride, val):
    if stride == 1:
        ref[pl.ds(start, BLK), :] = val
    else:
        ref[pl.ds(start, BLK, stride=stride), :] = val


def _fill_band_mask(mask_ref):
    qi = lax.broadcasted_iota(jnp.int32, (2 * BLK, 2 * BLK), 0) & (BLK - 1)
    ci = lax.broadcasted_iota(jnp.int32, (2 * BLK, 2 * BLK), 1)
    dist = BLK + qi - ci
    mask_ref[...] = jnp.where((dist >= 0) & (dist <= BLK), 0.0, NEG)


def _attn_chain(src, dst, mask_ref, chain, has_prev, order):
    q_ref, k_ref, v_ref, src0, src_stride, src_step = src
    a_ref, m_ref, l_ref, dst0, dst_stride, dst_step = dst
    lane = lax.broadcasted_iota(jnp.int32, (BLK, PAIR), 1)
    first = lane < HEAD_DIM
    lo = -1 if has_prev else 0
    k_f = [_rows(k_ref, src0 + j * src_step, src_stride) for j in range(lo, chain)]
    v_f = [_rows(v_ref, src0 + j * src_step, src_stride) for j in range(lo, chain)]
    k_t = [k.astype(BF16) for k in k_f]
    v_t = [v.astype(BF16) for v in v_f]
    nt = (((1,), (1,)), ((), ()))
    accs = []
    for j in range(chain):
        q = _rows(q_ref, src0 + j * src_step, src_stride)
        if order is not None:
            q = jnp.concatenate([q[:SUBLANES] + order, q[SUBLANES:]], axis=0)
        row = dst0 + j * dst_step
        t = j - lo
        if t > 0:
            q2 = jnp.concatenate([jnp.where(first, q, 0.0), jnp.where(first, 0.0, q)], axis=0).astype(BF16)
            k_all = jnp.concatenate([k_t[t - 1], k_t[t]], axis=0)
            v_all = jnp.concatenate([v_t[t - 1], v_t[t]], axis=0)
            v_aug = jnp.concatenate([v_all, jnp.ones(v_all.shape, BF16)], axis=1)
            s = lax.dot_general(q2, k_all, nt, preferred_element_type=F32) + mask_ref[...]
            m = jnp.max(s, axis=-1, keepdims=True)
            p = jnp.exp2(s - m).astype(BF16)
            pv = jnp.dot(p, v_aug, preferred_element_type=F32)
            acc = jnp.where(first, pv[:BLK, :PAIR], pv[BLK:, :PAIR])
            den = jnp.where(first, pv[:BLK, PAIR:], pv[BLK:, PAIR:])
            top = jnp.where(first, m[:BLK], m[BLK:])
        else:
            one = jnp.where(first, 1.0, 0.0)
            k_all = jnp.concatenate([jnp.where(first, k_f[0], 0.0), jnp.where(first, 0.0, k_f[0])], axis=0)
            v_all = jnp.concatenate([jnp.where(first, v_f[0], 0.0), jnp.where(first, 0.0, v_f[0])], axis=0)
            v_aug = jnp.concatenate([v_all, jnp.concatenate([one, 1.0 - one], axis=0)], axis=1).astype(BF16)
            own = mask_ref[:BLK, BLK:]
            s = lax.dot_general(q.astype(BF16), k_all.astype(BF16), nt, preferred_element_type=F32)
            s_a, s_b = s[:, :BLK] + own, s[:, BLK:] + own
            m_a = jnp.max(s_a, axis=-1, keepdims=True)
            m_b = jnp.max(s_b, axis=-1, keepdims=True)
            p = jnp.concatenate([jnp.exp2(s_a - m_a), jnp.exp2(s_b - m_b)], axis=1).astype(BF16)
            pv = jnp.dot(p, v_aug, preferred_element_type=F32)
            acc, den, top = pv[:, :PAIR], pv[:, PAIR:], jnp.where(first, m_a, m_b)
        _store_rows(a_ref, row, dst_stride, acc)
        _store_rows(l_ref, row, dst_stride, den)
        _store_rows(m_ref, row, dst_stride, top)
        accs.append(acc)
    return _zero_after(jnp.concatenate(accs, axis=0))


def _attn_kernel(q_ref, k_ref, v_ref, out_ref, mask_ref, qp_ref, kp_ref, vp_ref, *stats):
    seq = q_ref.shape[0]
    _fill_band_mask(mask_ref)
    for src_ref, plane_ref in ((q_ref, qp_ref), (k_ref, kp_ref), (v_ref, vp_ref)):
        for p in range(ATTN_PLANES):
            for c in range(0, seq // ATTN_PLANES, BLK):
                plane_ref[p, c:c + BLK, :] = src_ref[pl.ds(ATTN_PLANES * c + p, BLK, stride=ATTN_PLANES), :]

    def chains_of(d, plane):
        dilation = DILATIONS[d]
        dst_refs = stats[3 * d:3 * d + 3]
        span = BLK * dilation
        n_blocks = seq // span
        chain = min(n_blocks, ATTN_GROUP)
        chains = []
        for r in range(dilation):
            if plane is not None and r % ATTN_PLANES != plane:
                continue
            for c in range(n_blocks // chain):
                start = c * chain * span + r
                if plane is None:
                    src = (q_ref, k_ref, v_ref, start, dilation, span)
                    dst = dst_refs + (start, dilation, span)
                else:
                    where = (start // ATTN_PLANES, dilation // ATTN_PLANES, span // ATTN_PLANES)
                    src = (qp_ref.at[plane], kp_ref.at[plane], vp_ref.at[plane]) + where
                    dst = tuple(ref.at[plane] for ref in dst_refs) + where
                chains.append(functools.partial(_attn_chain, src, dst, mask_ref, chain, c > 0))
        per_piece = ATTN_GROUP // chain

        def run(order, group):
            return functools.reduce(jnp.add, [one(order) for one in group])
        return [functools.partial(run, group=chains[g:g + per_piece]) for g in range(0, len(chains), per_piece)]

    def mix_piece(order, plane):
        per_plane = seq // ATTN_PLANES

        def rows_of(ref, dilation):
            if dilation % ATTN_PLANES:
                return ref[pl.ds(plane, per_plane, stride=ATTN_PLANES), :]
            return ref[plane]
        accs, ms, sums = ([rows_of(stats[3 * d + i], dil) for d, dil in enumerate(DILATIONS)] for i in range(3))
        if order is not None:
            ms[0] = ms[0] + order
        top = functools.reduce(jnp.maximum, ms)
        es = [jnp.exp2(m - top) for m in ms]
        num = functools.reduce(jnp.add, [e * a for e, a in zip(es, accs)])
        den = functools.reduce(jnp.add, [e * l for e, l in zip(es, sums)])
        out = num / den
        out_ref[pl.ds(plane, per_plane, stride=ATTN_PLANES), :] = out
        return _zero_after(out)

    natural = [piece for d, dil in enumerate(DILATIONS) if dil % ATTN_PLANES for piece in chains_of(d, None)]
    planes = [[piece for d, dil in enumerate(DILATIONS) if dil % ATTN_PLANES == 0 for piece in chains_of(d, p)]
              for p in range(ATTN_PLANES)]
    share = -(-len(natural) // (ATTN_PLANES // 2))
    pieces, mixes = [], []
    for p in range(ATTN_PLANES):
        pieces += natural[p * share:(p + 1) * share] + planes[p]
        mixes.append(functools.partial(mix_piece, plane=p))
        if (p + 1) * share >= len(natural):
            pieces += mixes
            mixes = []
    orders = []
    for piece in pieces:
        orders.append(piece(orders[-ATTN_IN_FLIGHT] if len(orders) >= ATTN_IN_FLIGHT else None))


def _dilated_attn(q, k, v):
    B, S, _ = q.shape
    spec = pl.BlockSpec((None, S, PAIR), lambda b, h: (b, 0, h))
    planes = pltpu.VMEM((ATTN_PLANES, S // ATTN_PLANES, PAIR), F32)
    return pl.pallas_call(
        _attn_kernel,
        grid=(B, N_PAIRS),
        in_specs=[spec, spec, spec],
        out_specs=spec,
        out_shape=jax.ShapeDtypeStruct((B, S, D_ATTN), F32),
        scratch_shapes=[pltpu.VMEM((2 * BLK, 2 * BLK), F32)] + [planes] * 3
                       + [pltpu.VMEM((S, PAIR), F32) if d % ATTN_PLANES else planes
                          for d in DILATIONS for _ in range(3)],
        compiler_params=pltpu.CompilerParams(dimension_semantics=("arbitrary", "arbitrary"),
                                             vmem_limit_bytes=VMEM_LIMIT_BYTES),
        name="dilated_attn",
    )(q, k, v)


def _outproj_ffn_kernel(h1_ref, attn_ref, conv_ref, woa_ref, woc_ref, n2_ref, wg_ref, wu_ref, wd_ref,
                        out_ref):
    mix = (jnp.dot(attn_ref[...].astype(BF16), woa_ref[...], preferred_element_type=F32)
           + jnp.dot(conv_ref[...], woc_ref[...], preferred_element_type=F32))
    h2 = h1_ref[...] + mix
    out_ref[...] = h2 + 0.5 * _swiglu(_rms(h2, n2_ref[...]).astype(BF16), wg_ref, wu_ref, wd_ref)


def _outproj_ffn(h1, attn, conv, woa, woc, n2, wg, wu, wd):
    T = h1.shape[0]
    tm = TM_OUT
    row = lambda w: pl.BlockSpec((tm, w), lambda i: (i, 0))
    return pl.pallas_call(
        _outproj_ffn_kernel,
        grid=(T // tm,),
        in_specs=[row(D_MODEL), row(D_ATTN), row(D_CONV),
                  _const_spec((D_ATTN, D_MODEL)), _const_spec((D_CONV, D_MODEL)), _const_spec((1, D_MODEL)),
                  _const_spec((D_MODEL, D_FF)), _const_spec((D_MODEL, D_FF)), _const_spec((D_FF, D_MODEL))],
        out_specs=row(D_MODEL),
        out_shape=jax.ShapeDtypeStruct((T, D_MODEL), F32),
        compiler_params=pltpu.CompilerParams(dimension_semantics=("arbitrary",),
                                             vmem_limit_bytes=VMEM_LIMIT_BYTES),
        name="outproj_ffn",
    )(h1, attn, conv, woa, woc, n2, wg, wu, wd)


def _layer(h, ffn1_norm, ffn1_w_gate, ffn1_w_up, ffn1_w_down, mix_norm, w_in, q_norm, k_norm, conv_w,
           conv_b, conv_ln_g, conv_ln_b, w_out, ffn2_norm, ffn2_w_gate, ffn2_w_up, ffn2_w_down):
    B, S, D = h.shape
    T = B * S
    row = lambda a: a.reshape(1, -1).astype(F32)
    head_id = jnp.arange(D_ATTN) // HEAD_DIM
    head_mean = jnp.where(head_id[:, None] == head_id[None, :], 1.0 / HEAD_DIM, 0.0).astype(BF16)

    h1, q, k, v, conv = _ffn_inproj(
        h.reshape(T, D), S, row(ffn1_norm), ffn1_w_gate.astype(BF16), ffn1_w_up.astype(BF16),
        ffn1_w_down.astype(F32), row(mix_norm), w_in.astype(BF16),
        row(jnp.tile(q_norm, HEADS)), row(jnp.tile(k_norm, HEADS)), head_mean,
        conv_w.astype(F32), row(conv_b), row(conv_ln_g), row(conv_ln_b))

    attn = _dilated_attn(q.reshape(B, S, D_ATTN), k.reshape(B, S, D_ATTN), v.reshape(B, S, D_ATTN))

    w_out16 = w_out.astype(BF16)
    out = _outproj_ffn(h1, attn.reshape(T, D_ATTN), conv, w_out16[:D_ATTN], w_out16[D_ATTN:],
                       row(ffn2_norm), ffn2_w_gate.astype(BF16), ffn2_w_up.astype(BF16),
                       ffn2_w_down.astype(F32))
    return out.reshape(B, S, D)


def kernel(x, ffn1_norm, ffn1_w_gate, ffn1_w_up, ffn1_w_down, mix_norm, w_in, q_norm, k_norm, conv_w, conv_b,
           conv_ln_g, conv_ln_b, w_out, ffn2_norm, ffn2_w_gate, ffn2_w_up, ffn2_w_down):
    assert x.shape[1] % (BLK * max(DILATIONS)) == 0 and x.shape[1] % TM_FFN == 0 and x.shape[2] == D_MODEL
    h = x
    for l in range(ffn1_norm.shape[0]):
        h = _layer(h, ffn1_norm[l], ffn1_w_gate[l], ffn1_w_up[l], ffn1_w_down[l], mix_norm[l], w_in[l],
                   q_norm[l], k_norm[l], conv_w[l], conv_b[l], conv_ln_g[l], conv_ln_b[l], w_out[l],
                   ffn2_norm[l], ffn2_w_gate[l], ffn2_w_up[l], ffn2_w_down[l])
    return h
```
